```python
import math
import jax, jax.numpy as jnp
from jax import lax
import numpy as np

D_MODEL = 4096
BATCH = 4
SEQ = 2048
DEPTH = 4
DEC_BATCH = 128
DEC_SEQ = 8
PAST_LEN = 16384
PAGE_SIZE = 128

N_HEADS = 8
DK = D_MODEL // (2 * N_HEADS)
DV = D_MODEL // N_HEADS
D_M = N_HEADS * DV
CHUNK = 64
FORGET_BIAS_LO = 3.0
FORGET_BIAS_HI = 6.0
POOL_WINDOWS = (2, 4, 8, 16)
N_POOL_GROUPS = len(POOL_WINDOWS)
D_POOL = D_MODEL
POOL_GW = D_POOL // N_POOL_GROUPS
POOL_BUF = max(POOL_WINDOWS) - 1
D_FF = int(math.ceil(8 * D_MODEL / 3 / 64)) * 64
EPS = 1e-6
SPLIT_SIZES = (N_HEADS * DK, N_HEADS * DK, D_M, D_M, N_HEADS, N_HEADS, D_POOL, D_MODEL, D_MODEL)
D_IN = sum(SPLIT_SIZES)
F_OFFSET = sum(SPLIT_SIZES[:5])

kernel_name = "mlstm_pool_macaron_hybrid_step"


def _split_points():
    pts, acc = [], 0
    for s in SPLIT_SIZES[:-1]:
        acc += s
        pts.append(acc)
    return tuple(pts)


def rmsnorm(x, g):
    xf = x.astype(jnp.float32)
    y = xf * lax.rsqrt(jnp.mean(xf * xf, axis=-1, keepdims=True) + EPS)
    return (y * g.astype(jnp.float32)).astype(x.dtype)


def swiglu(h, w_in, w_out):
    gate, up = jnp.split(h @ w_in, 2, axis=-1)
    return (jax.nn.silu(gate) * up) @ w_out


def mlstm_chunk_step(carry, inp):
    C, n, m = carry
    q, k, v, logi, logf = inp
    L = q.shape[2]
    b = jnp.cumsum(logf, axis=-1)
    a = logi - b
    m_t = b + jnp.maximum(m[..., None], lax.cummax(a, axis=2))
    mask = jnp.tril(jnp.ones((L, L), dtype=bool))
    log_d = b[..., :, None] + a[..., None, :] - m_t[..., :, None]
    d = jnp.exp(jnp.where(mask, log_d, -jnp.inf))
    inter = jnp.exp(b + m[..., None] - m_t)
    s = jnp.einsum('bhtd,bhsd->bhts', q, k) * d
    num = inter[..., None] * jnp.einsum('bhtd,bhde->bhte', q, C) + jnp.einsum('bhts,bhse->bhte', s, v)
    den = inter * jnp.einsum('bhtd,bhd->bht', q, n) + jnp.sum(s, axis=-1)
    h = num / jnp.maximum(jnp.abs(den), jnp.exp(-m_t))[..., None]
    m_end = m_t[..., -1]
    w_end = jnp.exp(b[..., -1:] + a - m_end[..., None])
    decay = jnp.exp(b[..., -1] + m - m_end)
    kw = k * w_end[..., None]
    C_new = decay[..., None, None] * C + jnp.einsum('bhsd,bhse->bhde', kw, v)
    n_new = decay[..., None] * n + jnp.sum(kw, axis=2)
    return (C_new, n_new, m_end), h


def mlstm(q, k, v, logi, logf, C0, n0, m0):
    B, S = q.shape[0], q.shape[1]
    L = math.gcd(S, CHUNK)
    NC = S // L

    def chunks(t):
        t = t.astype(jnp.float32).reshape((B, NC, L, N_HEADS) + t.shape[3:])
        perm = (1, 0, 3, 2) + tuple(range(4, t.ndim))
        return jnp.transpose(t, perm)

    carry0 = (C0.astype(jnp.float32), n0.astype(jnp.float32), m0.astype(jnp.float32))
    (C, n, m), h = lax.scan(mlstm_chunk_step, carry0,
                            (chunks(q), chunks(k), chunks(v), chunks(logi), chunks(logf)))
    h = jnp.transpose(h, (1, 0, 3, 2, 4)).reshape(B, S, N_HEADS, DV)
    return h, C, n, m


def causal_multipool(ext, pos0, S):
    B = ext.shape[0]
    P = POOL_BUF
    xf = ext.astype(jnp.float32)
    cs = jnp.concatenate([jnp.zeros((B, 1, D_POOL), jnp.float32), jnp.cumsum(xf, axis=1)], axis=1)
    pos = pos0 + jnp.arange(S)
    outs = []
    for g, w in enumerate(POOL_WINDOWS):
        lo_c, hi_c = g * POOL_GW, (g + 1) * POOL_GW
        hi = cs[:, P + 1:P + 1 + S, lo_c:hi_c]
        lo = cs[:, P + 1 - w:P + 1 - w + S, lo_c:hi_c]
        cnt = jnp.minimum(pos + 1, w).astype(jnp.float32)[None, :, None]
        outs.append((hi - lo) / cnt - xf[:, P:, lo_c:hi_c])
    return jnp.concatenate(outs, axis=-1).astype(ext.dtype)


def mixer(h, pos0, C0, n0, m0, pool_prefix, w_in, b_in, head_norm, w_pool, pool_scale, w_out):
    B, S, _ = h.shape
    z = h @ w_in + b_in
    zq, zk, zv, zo, zi, zf, zu, zgm, zgp = jnp.split(z, _split_points(), axis=-1)
    q = zq.reshape(B, S, N_HEADS, DK)
    k = zk.reshape(B, S, N_HEADS, DK) * (DK ** -0.5)
    v = zv.reshape(B, S, N_HEADS, DV)
    logi = zi.astype(jnp.float32)
    logf = jax.nn.log_sigmoid(zf.astype(jnp.float32))
    hm, C, n, m = mlstm(q, k, v, logi, logf, C0, n0, m0)
    hm = rmsnorm(hm, head_norm.reshape(N_HEADS, DV)).reshape(B, S, D_M)
    hm = (hm * jax.nn.sigmoid(zo.astype(jnp.float32))).astype(h.dtype)
    ext = jnp.concatenate([pool_prefix.astype(zu.dtype), zu], axis=1)
    pooled = causal_multipool(ext, pos0, S)
    pool_out = jnp.einsum('bsgc,gcd->bsgd', pooled.reshape(B, S, N_POOL_GROUPS, POOL_GW), w_pool)
    pool_out = pool_out.reshape(B, S, D_POOL) * pool_scale
    merged = jax.nn.sigmoid(zgm) * hm + jax.nn.sigmoid(zgp) * pool_out
    return merged @ w_out, C, n, m, ext[:, -POOL_BUF:]


def trunk(x, pos0, C0, n0, m0, pool0, norm_ff1, w_ff1_in, w_ff1_out, norm_mix, w_in, b_in,
          head_norm, w_pool, pool_scale, w_out, norm_ff2, w_ff2_in, w_ff2_out, norm_final):
    Cs, ns, ms, ps = [], [], [], []
    for l in range(DEPTH):
        x = x + 0.5 * swiglu(rmsnorm(x, norm_ff1[l]), w_ff1_in[l], w_ff1_out[l])
        out, C, n, m, pb = mixer(rmsnorm(x, norm_mix[l]), pos0, C0[l], n0[l], m0[l], pool0[l],
                                 w_in[l], b_in[l], head_norm[l], w_pool[l], pool_scale[l], w_out[l])
        x = x + out.astype(x.dtype)
        x = x + 0.5 * swiglu(rmsnorm(x, norm_ff2[l]), w_ff2_in[l], w_ff2_out[l])
        Cs.append(C); ns.append(n); ms.append(m); ps.append(pb)
    return rmsnorm(x, norm_final), jnp.stack(Cs), jnp.stack(ns), jnp.stack(ms), jnp.stack(ps)


def setup_inputs(seed: int = 0) -> dict:
    key = jax.random.key(seed)
    ks = jax.random.split(key, 24)
    f32 = jnp.float32
    nrm = lambda k, shape, s: jax.random.normal(k, shape, f32) * s
    gain = lambda k, shape: 1.0 + 0.02 * jax.random.normal(k, shape, f32)
    b_in = nrm(ks[9], (DEPTH, D_IN), 0.01)
    f_off = jnp.linspace(FORGET_BIAS_LO, FORGET_BIAS_HI, N_HEADS, dtype=f32)
    b_in = b_in.at[:, F_OFFSET:F_OFFSET + N_HEADS].add(f_off[None, :])
    return {
        'x_prompt': nrm(ks[0], (BATCH, SEQ, D_MODEL), 1.0),
        'x_sample': nrm(ks[1], (DEC_BATCH, DEC_SEQ, D_MODEL), 1.0),
        'state_mlstm_C': nrm(ks[2], (DEPTH, DEC_BATCH, N_HEADS, DK, DV), 0.1),
        'state_mlstm_n': nrm(ks[3], (DEPTH, DEC_BATCH, N_HEADS, DK), 0.1),
        'state_mlstm_m': nrm(ks[4], (DEPTH, DEC_BATCH, N_HEADS), 1.0),
        'state_pool': nrm(ks[5], (DEPTH, DEC_BATCH, POOL_BUF, D_POOL), 1.0),
        'norm_ff1': gain(ks[6], (DEPTH, D_MODEL)),
        'w_ff1_in': nrm(ks[7], (DEPTH, D_MODEL, 2 * D_FF), D_MODEL ** -0.5),
        'w_ff1_out': nrm(ks[8], (DEPTH, D_FF, D_MODEL), D_FF ** -0.5),
        'norm_mix': gain(ks[10], (DEPTH, D_MODEL)),
        'w_in': nrm(ks[11], (DEPTH, D_MODEL, D_IN), D_MODEL ** -0.5),
        'b_in': b_in,
        'head_norm': gain(ks[12], (DEPTH, D_M)),
        'w_pool': nrm(ks[13], (DEPTH, N_POOL_GROUPS, POOL_GW, POOL_GW), POOL_GW ** -0.5),
        'pool_scale': 1.0 + 0.1 * jax.random.normal(ks[14], (DEPTH, D_POOL), f32),
        'w_out': nrm(ks[15], (DEPTH, D_MODEL, D_MODEL), D_MODEL ** -0.5),
        'norm_ff2': gain(ks[16], (DEPTH, D_MODEL)),
        'w_ff2_in': nrm(ks[17], (DEPTH, D_MODEL, 2 * D_FF), D_MODEL ** -0.5),
        'w_ff2_out': nrm(ks[18], (DEPTH, D_FF, D_MODEL), D_FF ** -0.5),
        'norm_final': gain(ks[19], (D_MODEL,)),
    }


def reference(x_prompt, x_sample, state_mlstm_C, state_mlstm_n, state_mlstm_m, state_pool,
              norm_ff1, w_ff1_in, w_ff1_out, norm_mix, w_in, b_in, head_norm, w_pool, pool_scale,
              w_out, norm_ff2, w_ff2_in, w_ff2_out, norm_final):
    B = x_prompt.shape[0]
    C0 = jnp.zeros((DEPTH, B, N_HEADS, DK, DV), jnp.float32)
    n0 = jnp.zeros((DEPTH, B, N_HEADS, DK), jnp.float32)
    m0 = jnp.zeros((DEPTH, B, N_HEADS), jnp.float32)
    pool0 = jnp.zeros((DEPTH, B, POOL_BUF, D_POOL), x_prompt.dtype)
    y_prompt, C_p, n_p, m_p, pool_p = trunk(
        x_prompt, 0, C0, n0, m0, pool0, norm_ff1, w_ff1_in, w_ff1_out, norm_mix, w_in, b_in,
        head_norm, w_pool, pool_scale, w_out, norm_ff2, w_ff2_in, w_ff2_out, norm_final)
    y_sample, C_s, n_s, m_s, pool_s = trunk(
        x_sample, PAST_LEN, state_mlstm_C, state_mlstm_n, state_mlstm_m, state_pool,
        norm_ff1, w_ff1_in, w_ff1_out, norm_mix, w_in, b_in,
        head_norm, w_pool, pool_scale, w_out, norm_ff2, w_ff2_in, w_ff2_out, norm_final)
    return (y_prompt, y_sample, C_p, n_p, m_p, pool_p, C_s, n_s, m_s, pool_s)
```

```python
import functools
import math

import jax
import jax.numpy as jnp
from jax import lax
from jax.experimental import pallas as pl
from jax.experimental.pallas import tpu as pltpu

F32 = jnp.float32
BF16 = jnp.bfloat16

D_MODEL = 4096
BATCH = 4
SEQ = 2048
DEPTH = 4
DEC_BATCH = 128
DEC_SEQ = 8
PAST_LEN = 16384
N_HEADS = 8
DK = D_MODEL // (2 * N_HEADS)
DV = D_MODEL // N_HEADS
POOL_WINDOWS = (2, 4, 8, 16)
POOL_GW = D_MODEL // len(POOL_WINDOWS)
POOL_BUF = max(POOL_WINDOWS) - 1
D_FF = int(math.ceil(8 * D_MODEL / 3 / 64)) * 64
EPS = 1e-6

T_PROMPT = BATCH * SEQ
T_SAMPLE = DEC_BATCH * DEC_SEQ
T_ALL = T_PROMPT + T_SAMPLE

LANES = 128
SUBLANES = 8
VMEM_LIMIT_BYTES = 56 * 1024 * 1024

D_FF_PAD = ((D_FF + 1023) // 1024) * 1024
TM = 1024
MLSTM_CHUNK = 256
HALO = POOL_BUF + 1


def _cparams(semantics):
    return pltpu.CompilerParams(dimension_semantics=semantics, vmem_limit_bytes=VMEM_LIMIT_BYTES)


def _sigmoid(x):
    return 1.0 / (1.0 + jnp.exp(-x))


def _log_sigmoid(x):
    return jnp.minimum(x, 0.0) - jnp.log1p(jnp.exp(-jnp.abs(x)))


def _rmsnorm_body(x_ref, g_ref, o_ref):
    x = x_ref[...]
    y = x * lax.rsqrt(jnp.mean(x * x, axis=-1, keepdims=True) + EPS)
    o_ref[...] = (y * g_ref[...]).astype(o_ref.dtype)


def _rmsnorm(x, g, out_dtype, tr=256):
    t, d = x.shape
    return pl.pallas_call(
        _rmsnorm_body,
        out_shape=jax.ShapeDtypeStruct((t, d), out_dtype),
        grid=(t // tr,),
        in_specs=[pl.BlockSpec((tr, d), lambda i: (i, 0)),
                  pl.BlockSpec((1, d), lambda i: (0, 0))],
        out_specs=pl.BlockSpec((tr, d), lambda i: (i, 0)),
        compiler_params=_cparams(("parallel",)),
        name="rmsnorm",
    )(x, g.reshape(1, d))


def _mm_bias_body(x_ref, w_ref, b_ref, o_ref):
    acc = jnp.dot(x_ref[...], w_ref[...], preferred_element_type=F32)
    o_ref[...] = (acc + b_ref[...]).astype(o_ref.dtype)


def _mm_bias(x, w_stack, layer, b, out_dtype, tn):
    t, k = x.shape
    n = w_stack.shape[2]
    return pl.pallas_call(
        _mm_bias_body,
        out_shape=jax.ShapeDtypeStruct((t, n), out_dtype),
        grid=(t // TM, n // tn),
        in_specs=[pl.BlockSpec((TM, k), lambda i, j: (i, 0)),
                  pl.BlockSpec((None, k, tn), lambda i, j: (layer, 0, j)),
                  pl.BlockSpec((1, tn), lambda i, j: (0, j))],
        out_specs=pl.BlockSpec((TM, tn), lambda i, j: (i, j)),
        compiler_params=_cparams(("parallel", "arbitrary")),
        name="in_proj",
    )(x, w_stack, b)


def _mm_resid_body(x_ref, w_ref, r_ref, o_ref):
    acc = jnp.dot(x_ref[...], w_ref[...], preferred_element_type=F32)
    o_ref[...] = r_ref[...] + acc


def _mm_resid(x, w_stack, layer, resid, tn=512):
    t, k = x.shape
    n = w_stack.shape[2]
    return pl.pallas_call(
        _mm_resid_body,
        out_shape=jax.ShapeDtypeStruct((t, n), F32),
        grid=(t // TM, n // tn),
        in_specs=[pl.BlockSpec((TM, k), lambda i, j: (i, 0)),
                  pl.BlockSpec((None, k, tn), lambda i, j: (layer, 0, j)),
                  pl.BlockSpec((TM, tn), lambda i, j: (i, j))],
        out_specs=pl.BlockSpec((TM, tn), lambda i, j: (i, j)),
        compiler_params=_cparams(("parallel", "arbitrary")),
        name="out_proj",
    )(x, w_stack, resid)


def _ffn_in_body(x_ref, wg_ref, wu_ref, o_ref):
    x = x_ref[...]
    g = jnp.dot(x, wg_ref[...], preferred_element_type=F32)
    u = jnp.dot(x, wu_ref[...], preferred_element_type=F32)
    o_ref[...] = (g * _sigmoid(g) * u).astype(o_ref.dtype)


def _ffn_in(x, wg_stack, wu_stack, layer, tn=512):
    t, k = x.shape
    n = wg_stack.shape[2]
    return pl.pallas_call(
        _ffn_in_body,
        out_shape=jax.ShapeDtypeStruct((t, n), BF16),
        grid=(t // TM, n // tn),
        in_specs=[pl.BlockSpec((TM, k), lambda i, j: (i, 0)),
                  pl.BlockSpec((None, k, tn), lambda i, j: (layer, 0, j)),
                  pl.BlockSpec((None, k, tn), lambda i, j: (layer, 0, j))],
        out_specs=pl.BlockSpec((TM, tn), lambda i, j: (i, j)),
        compiler_params=_cparams(("parallel", "arbitrary")),
        name="ffn_in",
    )(x, wg_stack, wu_stack)


def _ffn_out_body(a_ref, w_ref, r_ref, o_ref, acc_ref, *, nk):
    k = pl.program_id(2)

    @pl.when(k == 0)
    def _():
        acc_ref[...] = jnp.zeros_like(acc_ref)

    acc_ref[...] += jnp.dot(a_ref[...], w_ref[...], preferred_element_type=F32)

    @pl.when(k == nk - 1)
    def _():
        o_ref[...] = r_ref[...] + 0.5 * acc_ref[...]


def _ffn_out(a, w_stack, layer, resid, tn=1024, nk=4):
    t, kdim = a.shape
    n = w_stack.shape[2]
    tk = kdim // nk
    return pl.pallas_call(
        functools.partial(_ffn_out_body, nk=nk),
        out_shape=jax.ShapeDtypeStruct((t, n), F32),
        grid=(t // TM, n // tn, nk),
        in_specs=[pl.BlockSpec((TM, tk), lambda i, j, k: (i, k)),
                  pl.BlockSpec((None, tk, tn), lambda i, j, k: (layer, k, j)),
                  pl.BlockSpec((TM, tn), lambda i, j, k: (i, j))],
        out_specs=pl.BlockSpec((TM, tn), lambda i, j, k: (i, j)),
        scratch_shapes=[pltpu.VMEM((TM, tn), F32)],
        compiler_params=_cparams(("parallel", "arbitrary", "arbitrary")),
        name="ffn_out",
    )(a, w_stack, resid)


def _pool_merge_body(p_ref, w_ref, ps_ref, zgp_ref, hmg_ref, o_ref):
    acc = jnp.dot(p_ref[...].astype(BF16), w_ref[...], preferred_element_type=F32)
    o_ref[...] = (hmg_ref[...] + _sigmoid(zgp_ref[...]) * (acc * ps_ref[...])).astype(o_ref.dtype)


def _pool_merge(pooled, w_pool, layer, pool_scale, z_f, hmg):
    t = pooled.shape[0]
    ngroups = len(POOL_WINDOWS)
    zgp_col0 = 3 * D_MODEL // POOL_GW
    return pl.pallas_call(
        _pool_merge_body,
        out_shape=jax.ShapeDtypeStruct((t, D_MODEL), BF16),
        grid=(t // TM, ngroups),
        in_specs=[pl.BlockSpec((TM, POOL_GW), lambda i, g: (i, g)),
                  pl.BlockSpec((None, None, POOL_GW, POOL_GW), lambda i, g: (layer, g, 0, 0)),
                  pl.BlockSpec((1, POOL_GW), lambda i, g: (0, g)),
                  pl.BlockSpec((TM, POOL_GW), lambda i, g: (i, zgp_col0 + g)),
                  pl.BlockSpec((TM, POOL_GW), lambda i, g: (i, g))],
        out_specs=pl.BlockSpec((TM, POOL_GW), lambda i, g: (i, g)),
        compiler_params=_cparams(("parallel", "arbitrary")),
        name="pool_merge",
    )(pooled, w_pool, pool_scale, z_f, hmg)


def _pool_body(*refs, ts, bb, pos0, use_prev):
    if use_prev:
        cur_ref, prev_ref, pre_ref, o_ref, ext_ref = refs
    else:
        cur_ref, pre_ref, o_ref, ext_ref = refs
    t = pl.program_id(1)
    g = pl.program_id(2)
    rows = lax.broadcasted_iota(jnp.int32, (ts, 1), 0)
    for i in range(bb):
        if use_prev:
            @pl.when(t == 0)
            def _():
                ext_ref[0:HALO, :] = pre_ref[i]

            @pl.when(t > 0)
            def _():
                ext_ref[0:HALO, :] = prev_ref[...]
        else:
            ext_ref[0:HALO, :] = pre_ref[i]
        ext_ref[HALO:HALO + ts, :] = cur_ref[i * ts:(i + 1) * ts, :]

        for gi, w in enumerate(POOL_WINDOWS):
            @pl.when(g == gi)
            def _():
                cur = ext_ref[pl.ds(HALO, ts), :]
                acc = cur
                for j in range(1, w):
                    acc = acc + ext_ref[pl.ds(HALO - j, ts), :]
                cnt = jnp.minimum(pos0 + t * ts + rows + 1, w).astype(F32)
                o_ref[i * ts:(i + 1) * ts, :] = acc / cnt - cur


def _pool(z_f, row0, nbatch, seq, ts, bb, pos0, prefix):
    nt = seq // ts
    use_prev = nt > 1
    tc = POOL_GW
    u_col0 = D_MODEL // tc
    rb0 = row0 // (ts * bb)
    in_specs = [pl.BlockSpec((ts * bb, tc), lambda b, t, g: (rb0 + b * nt + t, u_col0 + g))]
    args = [z_f]
    if use_prev:
        per = ts // HALO
        in_specs.append(pl.BlockSpec(
            (HALO, tc), lambda b, t, g: (jnp.maximum((rb0 + b * nt + t) * per - 1, 0), u_col0 + g)))
        args.append(z_f)
    in_specs.append(pl.BlockSpec((bb, HALO, tc), lambda b, t, g: (b, 0, g)))
    args.append(prefix)
    return pl.pallas_call(
        functools.partial(_pool_body, ts=ts, bb=bb, pos0=pos0, use_prev=use_prev),
        out_shape=jax.ShapeDtypeStruct((nbatch * seq, D_MODEL), F32),
        grid=(nbatch // bb, nt, len(POOL_WINDOWS)),
        in_specs=in_specs,
        out_specs=pl.BlockSpec((ts * bb, tc), lambda b, t, g: (b * nt + t, g)),
        scratch_shapes=[pltpu.VMEM((HALO + ts, tc), F32)],
        compiler_params=_cparams(("parallel", "arbitrary", "arbitrary")),
        name="pool",
    )(*args)


def _mlstm_body(*refs, L, LP, HB, has_state):
    if has_state:
        (q_ref, k_ref, v_ref, zo_ref, zgm_ref, zic_ref, zfc_ref, zir_ref, zfr_ref, hn_ref,
         c0_ref, n0_ref, m0_ref, h_ref, c_ref, n_ref, m_ref) = refs
    else:
        (q_ref, k_ref, v_ref, zo_ref, zgm_ref, zic_ref, zfc_ref, zir_ref, zfr_ref, hn_ref,
         h_ref, c_ref, n_ref, m_ref) = refs
    hb = pl.program_id(1)
    c = pl.program_id(2)

    @pl.when(c == 0)
    def _():
        if has_state:
            c_ref[...] = c0_ref[...]
            n_ref[...] = n0_ref[...]
            m_ref[...] = m0_ref[...]
        else:
            c_ref[...] = jnp.zeros_like(c_ref)
            n_ref[...] = jnp.zeros_like(n_ref)
            m_ref[...] = jnp.zeros_like(m_ref)

    t_ids = lax.broadcasted_iota(jnp.int32, (LP, LP), 0)
    s_ids = lax.broadcasted_iota(jnp.int32, (LP, LP), 1)
    tri = s_ids <= t_ids
    tri_t = t_ids <= s_ids
    head_lane = lax.broadcasted_iota(jnp.int32, (L, N_HEADS), 1)
    head_sub = lax.broadcasted_iota(jnp.int32, (N_HEADS, LP), 0)
    zi_c = zic_ref[...]
    zf_c = zfc_ref[...]
    zi_r = zir_ref[0]
    zf_r = zfr_ref[0]
    pad = LP - L

    def pad_rows(x, value):
        if pad == 0:
            return x
        return jnp.concatenate([x, jnp.full((pad, x.shape[1]), value, x.dtype)], axis=0)

    for j in range(HB):
        head = hb * HB + j
        logi_c = pad_rows(jnp.sum(jnp.where(head_lane == head, zi_c, 0.0), axis=1, keepdims=True), -jnp.inf)
        logf_c = pad_rows(_log_sigmoid(jnp.sum(jnp.where(head_lane == head, zf_c, 0.0), axis=1, keepdims=True)), 0.0)
        logi_r = jnp.sum(jnp.where(head_sub == head, zi_r, 0.0), axis=0, keepdims=True)
        logf_r = _log_sigmoid(jnp.sum(jnp.where(head_sub == head, zf_r, 0.0), axis=0, keepdims=True))

        b_c = jnp.sum(jnp.where(tri, logf_r, 0.0), axis=1, keepdims=True)
        b_r = jnp.sum(jnp.where(tri_t, logf_c, 0.0), axis=0, keepdims=True)
        b_last = jnp.sum(logf_r, axis=1, keepdims=True)
        a_c = logi_c - b_c
        a_r = logi_r - b_r
        m_prev = m_ref[0, j]
        big_m = jnp.maximum(m_prev, jnp.max(jnp.where(tri, a_r, -jnp.inf), axis=1, keepdims=True))
        m_last = jnp.maximum(m_prev, jnp.max(a_r, axis=1, keepdims=True))
        d = jnp.exp(jnp.where(tri, a_r - big_m, -jnp.inf))
        inter = jnp.exp(m_prev - big_m)

        q = pad_rows(q_ref[:, j * DK:(j + 1) * DK], 0.0).astype(BF16)
        k = pad_rows(k_ref[:, j * DK:(j + 1) * DK], 0.0)
        v = pad_rows(v_ref[:, j * DV:(j + 1) * DV], 0.0).astype(BF16)
        cmat = c_ref[0, j]
        n_row = n_ref[0, j]

        qk = lax.dot_general(q, k.astype(BF16), (((1,), (1,)), ((), ())), preferred_element_type=F32)
        s = qk * (d * (DK ** -0.5))
        num = inter * jnp.dot(q, cmat.astype(BF16), preferred_element_type=F32)
        num = num + jnp.dot(s.astype(BF16), v, preferred_element_type=F32)
        qn = jnp.sum(q.astype(F32) * n_row, axis=1, keepdims=True)
        den = inter * qn + jnp.sum(s, axis=1, keepdims=True)
        m_t = b_c + big_m
        hval = num * (1.0 / jnp.maximum(jnp.abs(den), jnp.exp(-m_t)))

        w_end = jnp.exp(a_c - m_last)
        decay = jnp.exp(m_prev - m_last)
        kw = k.astype(F32) * (w_end * (DK ** -0.5))
        kv = lax.dot_general(kw.astype(BF16), v, (((0,), (0,)), ((), ())), preferred_element_type=F32)
        c_ref[0, j] = decay * cmat + kv
        n_ref[0, j] = decay * n_row + jnp.sum(kw, axis=0, keepdims=True)
        m_ref[0, j] = b_last + m_last

        hval = hval[:L]
        hnorm = hval * lax.rsqrt(jnp.mean(hval * hval, axis=-1, keepdims=True) + EPS)
        hnorm = hnorm * hn_ref[:, j * DV:(j + 1) * DV]
        hm = hnorm * _sigmoid(zo_ref[:, j * DV:(j + 1) * DV])
        h_ref[:, j * DV:(j + 1) * DV] = _sigmoid(zgm_ref[:, j * DV:(j + 1) * DV]) * hm


def _mlstm(qkv, z_f, zi_col, zf_col, zi_row, zf_row, head_norm, *, nbatch, seq, L, LP, HB, row0, state=None):
    nc = seq // L
    nhb = N_HEADS // HB
    zrow0 = row0 // L
    in_specs = [
        pl.BlockSpec((L, HB * DK), lambda b, h, c: (b * nc + c, h)),
        pl.BlockSpec((L, HB * DK), lambda b, h, c: (b * nc + c, nhb + h)),
        pl.BlockSpec((L, HB * DV), lambda b, h, c: (b * nc + c, nhb + h)),
        pl.BlockSpec((L, HB * DV), lambda b, h, c: (zrow0 + b * nc + c, h)),
        pl.BlockSpec((L, HB * DV), lambda b, h, c: (zrow0 + b * nc + c, 2 * nhb + h)),
        pl.BlockSpec((L, N_HEADS), lambda b, h, c: (b * nc + c, 0)),
        pl.BlockSpec((L, N_HEADS), lambda b, h, c: (b * nc + c, 0)),
        pl.BlockSpec((1, N_HEADS, LP), lambda b, h, c: (b, 0, c)),
        pl.BlockSpec((1, N_HEADS, LP), lambda b, h, c: (b, 0, c)),
        pl.BlockSpec((1, HB * DV), lambda b, h, c: (0, h)),
    ]
    args = [qkv, qkv, qkv, z_f, z_f, zi_col, zf_col, zi_row, zf_row, head_norm]
    if state is not None:
        c0, n0, m0, layer = state
        in_specs += [
            pl.BlockSpec((None, 1, HB, DK, DV), lambda b, h, c: (layer, b, h, 0, 0)),
            pl.BlockSpec((None, 1, HB, 1, DK), lambda b, h, c: (layer, b, h, 0, 0)),
            pl.BlockSpec((None, 1, HB, 1, 1), lambda b, h, c: (layer, b, h, 0, 0)),
        ]
        args += [c0, n0, m0]
    out_shape = (
        jax.ShapeDtypeStruct((nbatch * seq, N_HEADS * DV), F32),
        jax.ShapeDtypeStruct((nbatch, N_HEADS, DK, DV), F32),
        jax.ShapeDtypeStruct((nbatch, N_HEADS, 1, DK), F32),
        jax.ShapeDtypeStruct((nbatch, N_HEADS, 1, 1), F32),
    )
    out_specs = (
        pl.BlockSpec((L, HB * DV), lambda b, h, c: (b * nc + c, h)),
        pl.BlockSpec((1, HB, DK, DV), lambda b, h, c: (b, h, 0, 0)),
        pl.BlockSpec((1, HB, 1, DK), lambda b, h, c: (b, h, 0, 0)),
        pl.BlockSpec((1, HB, 1, 1), lambda b, h, c: (b, h, 0, 0)),
    )
    return pl.pallas_call(
        functools.partial(_mlstm_body, L=L, LP=LP, HB=HB, has_state=state is not None),
        out_shape=out_shape,
        grid=(nbatch, nhb, nc),
        in_specs=in_specs,
        out_specs=out_specs,
        compiler_params=_cparams(("parallel", "parallel", "arbitrary")),
        name="mlstm_state" if state is not None else "mlstm",
    )(*args)


def _prep_weights(w_ff_in, w_ff_out, w_in, b_in, w_pool, w_out):
    pad_ff = D_FF_PAD - D_FF
    wg = jnp.pad(w_ff_in[:, :, :D_FF], ((0, 0), (0, 0), (0, pad_ff))).astype(BF16)
    wu = jnp.pad(w_ff_in[:, :, D_FF:], ((0, 0), (0, 0), (0, pad_ff))).astype(BF16)
    wo = jnp.pad(w_ff_out, ((0, 0), (0, pad_ff), (0, 0))).astype(BF16)
    out = {"wg": wg, "wu": wu, "wo": wo}
    if w_in is not None:
        nqkv = 2 * N_HEADS * DK + N_HEADS * DV
        ngate0 = nqkv + D_MODEL
        ngate1 = ngate0 + 2 * N_HEADS
        out["w_qkv"] = w_in[:, :, :nqkv].astype(BF16)
        out["w_f"] = jnp.concatenate([w_in[:, :, nqkv:ngate0], w_in[:, :, ngate1:]], axis=2).astype(BF16)
        out["w_if"] = jnp.pad(w_in[:, :, ngate0:ngate1], ((0, 0), (0, 0), (0, LANES - 2 * N_HEADS))).astype(BF16)
        out["b_qkv"] = b_in[:, None, :nqkv]
        out["b_f"] = jnp.concatenate([b_in[:, None, nqkv:ngate0], b_in[:, None, ngate1:]], axis=2)
        out["b_if"] = jnp.pad(b_in[:, None, ngate0:ngate1], ((0, 0), (0, 0), (0, LANES - 2 * N_HEADS)))
        out["w_pool"] = w_pool.astype(BF16)
        out["w_out"] = w_out.astype(BF16)
    return out


def _time_on_lanes(z, nbatch, seq, lp, fill):
    zr = jnp.transpose(z.reshape(nbatch, seq, N_HEADS), (0, 2, 1))
    if seq < lp:
        zr = jnp.pad(zr, ((0, 0), (0, 0), (0, lp - seq)), constant_values=fill)
    return zr


@jax.jit
def _forward(x_prompt, x_sample, state_c, state_n, state_m, state_pool, norm_ff1, w_ff1_in, w_ff1_out,
             norm_mix, w_in, b_in, head_norm, w_pool, pool_scale, w_out, norm_ff2, w_ff2_in, w_ff2_out,
             norm_final):
    ff1 = _prep_weights(w_ff1_in, w_ff1_out, w_in, b_in, w_pool, w_out)
    ff2 = _prep_weights(w_ff2_in, w_ff2_out, None, None, None, None)
    state_n5 = state_n.reshape(DEPTH, DEC_BATCH, N_HEADS, 1, DK)
    state_m5 = state_m.reshape(DEPTH, DEC_BATCH, N_HEADS, 1, 1)
    zero_prefix = jnp.zeros((BATCH, HALO, D_MODEL), F32)

    x = jnp.concatenate([x_prompt.reshape(T_PROMPT, D_MODEL), x_sample.reshape(T_SAMPLE, D_MODEL)], axis=0)
    cs_p, ns_p, ms_p, ps_p, cs_s, ns_s, ms_s, ps_s = ([] for _ in range(8))
    for l in range(DEPTH):
        a = _ffn_in(_rmsnorm(x, norm_ff1[l], BF16), ff1["wg"], ff1["wu"], l)
        x = _ffn_out(a, ff1["wo"], l, x)

        hn = _rmsnorm(x, norm_mix[l], BF16)
        qkv = _mm_bias(hn, ff1["w_qkv"], l, ff1["b_qkv"][l], BF16, 1024)
        z_f = _mm_bias(hn, ff1["w_f"], l, ff1["b_f"][l], F32, 1024)
        gates = _mm_bias(hn, ff1["w_if"], l, ff1["b_if"][l], F32, LANES)
        zi, zf = gates[:, :N_HEADS], gates[:, N_HEADS:2 * N_HEADS]
        hnorm_w = head_norm[l].reshape(1, N_HEADS * DV)

        hmg_p, c_p, n_p, m_p = _mlstm(
            qkv, z_f, zi[:T_PROMPT], zf[:T_PROMPT],
            _time_on_lanes(zi[:T_PROMPT], BATCH, SEQ, MLSTM_CHUNK, 0.0),
            _time_on_lanes(zf[:T_PROMPT], BATCH, SEQ, MLSTM_CHUNK, 0.0),
            hnorm_w, nbatch=BATCH, seq=SEQ, L=MLSTM_CHUNK, LP=MLSTM_CHUNK, HB=1, row0=0)
        hmg_s, c_s, n_s, m_s = _mlstm(
            qkv[T_PROMPT:].astype(F32), z_f, zi[T_PROMPT:], zf[T_PROMPT:],
            _time_on_lanes(zi[T_PROMPT:], DEC_BATCH, DEC_SEQ, LANES, -jnp.inf),
            _time_on_lanes(zf[T_PROMPT:], DEC_BATCH, DEC_SEQ, LANES, jnp.inf),
            hnorm_w, nbatch=DEC_BATCH, seq=DEC_SEQ, L=DEC_SEQ, LP=LANES, HB=N_HEADS, row0=T_PROMPT,
            state=(state_c, state_n5, state_m5, l))
        hmg = jnp.concatenate([hmg_p, hmg_s], axis=0)

        prefix_s = jnp.pad(state_pool[l], ((0, 0), (HALO - POOL_BUF, 0), (0, 0)))
        pooled_p = _pool(z_f, 0, BATCH, SEQ, 256, 1, 0, zero_prefix)
        pooled_s = _pool(z_f, T_PROMPT, DEC_BATCH, DEC_SEQ, DEC_SEQ, 16, PAST_LEN, prefix_s)
        pooled = jnp.concatenate([pooled_p, pooled_s], axis=0)

        merged = _pool_merge(pooled, ff1["w_pool"], l, pool_scale[l].reshape(1, D_MODEL), z_f, hmg)
        x = _mm_resid(merged, ff1["w_out"], l, x)

        a = _ffn_in(_rmsnorm(x, norm_ff2[l], BF16), ff2["wg"], ff2["wu"], l)
        x = _ffn_out(a, ff2["wo"], l, x)

        zu = z_f[:, D_MODEL:2 * D_MODEL]
        cs_p.append(c_p)
        ns_p.append(n_p.reshape(BATCH, N_HEADS, DK))
        ms_p.append(m_p.reshape(BATCH, N_HEADS))
        ps_p.append(zu[:T_PROMPT].reshape(BATCH, SEQ, D_MODEL)[:, SEQ - POOL_BUF:])
        cs_s.append(c_s)
        ns_s.append(n_s.reshape(DEC_BATCH, N_HEADS, DK))
        ms_s.append(m_s.reshape(DEC_BATCH, N_HEADS))
        ext_s = jnp.concatenate([state_pool[l], zu[T_PROMPT:].reshape(DEC_BATCH, DEC_SEQ, D_MODEL)], axis=1)
        ps_s.append(ext_s[:, ext_s.shape[1] - POOL_BUF:])

    y = _rmsnorm(x, norm_final, F32)
    return (y[:T_PROMPT].reshape(BATCH, SEQ, D_MODEL), y[T_PROMPT:].reshape(DEC_BATCH, DEC_SEQ, D_MODEL),
            jnp.stack(cs_p), jnp.stack(ns_p), jnp.stack(ms_p), jnp.stack(ps_p),
            jnp.stack(cs_s), jnp.stack(ns_s), jnp.stack(ms_s), jnp.stack(ps_s))


def kernel(x_prompt, x_sample, state_mlstm_C, state_mlstm_n, state_mlstm_m, state_pool, norm_ff1, w_ff1_in,
           w_ff1_out, norm_mix, w_in, b_in, head_norm, w_pool, pool_scale, w_out, norm_ff2, w_ff2_in, w_ff2_out,
           norm_final):
    return _forward(x_prompt, x_sample, state_mlstm_C, state_mlstm_n, state_mlstm_m, state_pool, norm_ff1,
                    w_ff1_in, w_ff1_out, norm_mix, w_in, b_in, head_norm, w_pool, pool_scale, w_out, norm_ff2,
                    w_ff2_in, w_ff2_out, norm_final)
```

```python
import functools
import math

import jax
import jax.numpy as jnp
from jax import lax
from jax.experimental import pallas as pl
from jax.experimental.pallas import tpu as pltpu

F32 = jnp.float32
BF16 = jnp.bfloat16

D_MODEL = 4096
BATCH = 4
SEQ = 2048
DEPTH = 4
DEC_BATCH = 128
DEC_SEQ = 8
PAST_LEN = 16384
N_HEADS = 8
DK = D_MODEL // (2 * N_HEADS)
DV = D_MODEL // N_HEADS
POOL_WINDOWS = (2, 4, 8, 16)
POOL_GW = D_MODEL // len(POOL_WINDOWS)
POOL_BUF = max(POOL_WINDOWS) - 1
D_FF = int(math.ceil(8 * D_MODEL / 3 / 64)) * 64
EPS = 1e-6

T_PROMPT = BATCH * SEQ
T_SAMPLE = DEC_BATCH * DEC_SEQ
T_ALL = T_PROMPT + T_SAMPLE

COL_QKV = 0
N_QKV = 2 * N_HEADS * DK + N_HEADS * DV
COL_O = N_QKV
COL_IF = COL_O + D_MODEL
COL_REST = COL_IF + 2 * N_HEADS
N_REST = 3 * D_MODEL

LANES = 128
SUBLANES = 8
MXU_WIDTH = 256
VMEM_LIMIT_BYTES = 56 * 1024 * 1024

D_FF_PAD = ((D_FF + MXU_WIDTH - 1) // MXU_WIDTH) * MXU_WIDTH
TM = 1024
TM_FFN_IN = 1536
MLSTM_CHUNK = 256
HALO = POOL_BUF + 1


def _cparams(semantics):
    return pltpu.CompilerParams(dimension_semantics=semantics, vmem_limit_bytes=VMEM_LIMIT_BYTES)


def _sigmoid(x):
    return 1.0 / (1.0 + jnp.exp(-x))


def _log_sigmoid(x):
    return jnp.minimum(x, 0.0) - jnp.log1p(jnp.exp(-jnp.abs(x)))


def _bf16(w):
    return w if w.dtype == BF16 else w.astype(BF16)


def _rmsnorm_body(x_ref, g_ref, o_ref):
    x = x_ref[...]
    y = x * lax.rsqrt(jnp.mean(x * x, axis=-1, keepdims=True) + EPS)
    o_ref[...] = (y * g_ref[...]).astype(o_ref.dtype)


def _rmsnorm(x, g, out_dtype, tr=256):
    t, d = x.shape
    return pl.pallas_call(
        _rmsnorm_body,
        out_shape=jax.ShapeDtypeStruct((t, d), out_dtype),
        grid=(t // tr,),
        in_specs=[pl.BlockSpec((tr, d), lambda i: (i, 0)),
                  pl.BlockSpec((1, d), lambda i: (0, 0))],
        out_specs=pl.BlockSpec((tr, d), lambda i: (i, 0)),
        compiler_params=_cparams(("parallel",)),
        name="rmsnorm",
    )(x, g.reshape(1, d))


def _mm_bias_body(x_ref, w_ref, b_ref, o_ref):
    acc = jnp.dot(x_ref[...], _bf16(w_ref[...]), preferred_element_type=F32)
    o_ref[...] = (acc + b_ref[...]).astype(o_ref.dtype)


def _mm_bias(x, w_stack, b_stack, layer, col0, ncols, out_dtype, tn):
    t, k = x.shape
    cb0 = col0 // tn
    return pl.pallas_call(
        _mm_bias_body,
        out_shape=jax.ShapeDtypeStruct((t, ncols), out_dtype),
        grid=(t // TM, ncols // tn),
        in_specs=[pl.BlockSpec((TM, k), lambda i, j: (i, 0)),
                  pl.BlockSpec((None, k, tn), lambda i, j: (layer, 0, cb0 + j)),
                  pl.BlockSpec((None, 1, tn), lambda i, j: (layer, 0, cb0 + j))],
        out_specs=pl.BlockSpec((TM, tn), lambda i, j: (i, j)),
        compiler_params=_cparams(("parallel", "arbitrary")),
        name="in_proj",
    )(x, w_stack, b_stack)


def _mm_resid_body(x_ref, w_ref, r_ref, o_ref):
    acc = jnp.dot(x_ref[...], _bf16(w_ref[...]), preferred_element_type=F32)
    o_ref[...] = r_ref[...] + acc


def _mm_resid(x, w_stack, layer, resid, tn=512):
    t, k = x.shape
    n = w_stack.shape[2]
    return pl.pallas_call(
        _mm_resid_body,
        out_shape=jax.ShapeDtypeStruct((t, n), F32),
        grid=(t // TM, n // tn),
        in_specs=[pl.BlockSpec((TM, k), lambda i, j: (i, 0)),
                  pl.BlockSpec((None, k, tn), lambda i, j: (layer, 0, j)),
                  pl.BlockSpec((TM, tn), lambda i, j: (i, j))],
        out_specs=pl.BlockSpec((TM, tn), lambda i, j: (i, j)),
        compiler_params=_cparams(("parallel", "arbitrary")),
        name="out_proj",
    )(x, w_stack, resid)


def _ffn_in_body(x_ref, wg_ref, wu_ref, o_ref):
    x = x_ref[...]
    g = jnp.dot(x, _bf16(wg_ref[...]), preferred_element_type=F32)
    u = jnp.dot(x, wu_ref[...], preferred_element_type=F32)
    o_ref[...] = (g * _sigmoid(g) * u).astype(o_ref.dtype)


def _ffn_in(x, w_in_stack, wu_stack, layer, tn=MXU_WIDTH):
    t, k = x.shape
    n = wu_stack.shape[2]
    return pl.pallas_call(
        _ffn_in_body,
        out_shape=jax.ShapeDtypeStruct((t, n), BF16),
        grid=(t // TM_FFN_IN, n // tn),
        in_specs=[pl.BlockSpec((TM_FFN_IN, k), lambda i, j: (i, 0)),
                  pl.BlockSpec((None, k, tn), lambda i, j: (layer, 0, j)),
                  pl.BlockSpec((None, k, tn), lambda i, j: (layer, 0, j))],
        out_specs=pl.BlockSpec((TM_FFN_IN, tn), lambda i, j: (i, j)),
        compiler_params=_cparams(("parallel", "arbitrary")),
        name="ffn_in",
    )(x, w_in_stack, wu_stack)


def _ffn_out_body(a_ref, w_ref, r_ref, o_ref, acc_ref, *, nk):
    k = pl.program_id(2)

    @pl.when(k == 0)
    def _():
        acc_ref[...] = jnp.zeros_like(acc_ref)

    acc_ref[...] += jnp.dot(a_ref[...], w_ref[...], preferred_element_type=F32)

    @pl.when(k == nk - 1)
    def _():
        o_ref[...] = r_ref[...] + 0.5 * acc_ref[...]


def _ffn_out(a, w_stack, layer, resid, tn=512, nk=2):
    t, kdim = a.shape
    n = w_stack.shape[2]
    tk = kdim // nk
    return pl.pallas_call(
        functools.partial(_ffn_out_body, nk=nk),
        out_shape=jax.ShapeDtypeStruct((t, n), F32),
        grid=(t // TM, n // tn, nk),
        in_specs=[pl.BlockSpec((TM, tk), lambda i, j, k: (i, k)),
                  pl.BlockSpec((None, tk, tn), lambda i, j, k: (layer, k, j)),
                  pl.BlockSpec((TM, tn), lambda i, j, k: (i, j))],
        out_specs=pl.BlockSpec((TM, tn), lambda i, j, k: (i, j)),
        scratch_shapes=[pltpu.VMEM((TM, tn), F32)],
        compiler_params=_cparams(("parallel", "arbitrary", "arbitrary")),
        name="ffn_out",
    )(a, w_stack, resid)


def _pool_merge_body(pp_ref, ps_ref, w_ref, sc_ref, zgp_ref, hp_ref, hs_ref, o_ref, *, n_prompt_tiles):
    i = pl.program_id(0)

    def merge(p_ref, h_ref):
        acc = jnp.dot(p_ref[...].astype(BF16), _bf16(w_ref[...]), preferred_element_type=F32)
        o_ref[...] = (h_ref[...] + _sigmoid(zgp_ref[...]) * (acc * sc_ref[...])).astype(o_ref.dtype)

    @pl.when(i < n_prompt_tiles)
    def _():
        merge(pp_ref, hp_ref)

    @pl.when(i >= n_prompt_tiles)
    def _():
        merge(ps_ref, hs_ref)


def _pool_merge(pooled_p, pooled_s, w_pool, layer, pool_scale, z_r, hmg_p, hmg_s):
    ngroups = len(POOL_WINDOWS)
    npt = T_PROMPT // TM
    assert T_SAMPLE == TM
    zgp_cb0 = 2 * D_MODEL // POOL_GW
    prompt_map = lambda i, g: (jnp.minimum(i, npt - 1), g)
    sample_map = lambda i, g: (0, jnp.where(i >= npt, g, 0))
    return pl.pallas_call(
        functools.partial(_pool_merge_body, n_prompt_tiles=npt),
        out_shape=jax.ShapeDtypeStruct((T_ALL, D_MODEL), BF16),
        grid=(T_ALL // TM, ngroups),
        in_specs=[pl.BlockSpec((TM, POOL_GW), prompt_map),
                  pl.BlockSpec((TM, POOL_GW), sample_map),
                  pl.BlockSpec((None, None, POOL_GW, POOL_GW), lambda i, g: (layer, g, 0, 0)),
                  pl.BlockSpec((None, 1, POOL_GW), lambda i, g: (layer, 0, g)),
                  pl.BlockSpec((TM, POOL_GW), lambda i, g: (i, zgp_cb0 + g)),
                  pl.BlockSpec((TM, POOL_GW), prompt_map),
                  pl.BlockSpec((TM, POOL_GW), sample_map)],
        out_specs=pl.BlockSpec((TM, POOL_GW), lambda i, g: (i, g)),
        compiler_params=_cparams(("parallel", "arbitrary")),
        name="pool_merge",
    )(pooled_p, pooled_s, w_pool, pool_scale, z_r, hmg_p, hmg_s)


def _pool_body(*refs, ts, bb, pos0, use_prev, has_prefix):
    refs = list(refs)
    cur_ref = refs.pop(0)
    prev_ref = refs.pop(0) if use_prev else None
    pre_ref = refs.pop(0) if has_prefix else None
    o_ref, ext_ref = refs
    t = pl.program_id(1)
    g = pl.program_id(2)
    rows = lax.broadcasted_iota(jnp.int32, (ts, 1), 0)
    lead = HALO - POOL_BUF
    for i in range(bb):
        def first_tile_halo():
            if has_prefix:
                ext_ref[lead:HALO, :] = pre_ref[i]
            else:
                ext_ref[0:HALO, :] = jnp.zeros((HALO, ext_ref.shape[1]), F32)

        if use_prev:
            pl.when(t == 0)(first_tile_halo)

            @pl.when(t > 0)
            def _():
                ext_ref[0:HALO, :] = prev_ref[...]
        else:
            first_tile_halo()
        ext_ref[HALO:HALO + ts, :] = cur_ref[i * ts:(i + 1) * ts, :]

        for gi, w in enumerate(POOL_WINDOWS):
            @pl.when(g == gi)
            def _():
                cur = ext_ref[pl.ds(HALO, ts), :]
                acc = cur
                for j in range(1, w):
                    acc = acc + ext_ref[pl.ds(HALO - j, ts), :]
                cnt = jnp.minimum(pos0 + t * ts + rows + 1, w).astype(F32)
                o_ref[i * ts:(i + 1) * ts, :] = acc / cnt - cur


def _pool(z_r, row0, nbatch, seq, ts, bb, pos0, prefix=None, layer=0):
    nt = seq // ts
    use_prev = nt > 1
    tc = POOL_GW
    rb0 = row0 // (ts * bb)
    in_specs = [pl.BlockSpec((ts * bb, tc), lambda b, t, g: (rb0 + b * nt + t, g))]
    args = [z_r]
    if use_prev:
        per = ts // HALO
        in_specs.append(pl.BlockSpec(
            (HALO, tc), lambda b, t, g: (jnp.maximum((rb0 + b * nt + t) * per - 1, 0), g)))
        args.append(z_r)
    if prefix is not None:
        in_specs.append(pl.BlockSpec((None, bb, POOL_BUF, tc), lambda b, t, g: (layer, b, 0, g)))
        args.append(prefix)
    return pl.pallas_call(
        functools.partial(_pool_body, ts=ts, bb=bb, pos0=pos0, use_prev=use_prev, has_prefix=prefix is not None),
        out_shape=jax.ShapeDtypeStruct((nbatch * seq, D_MODEL), F32),
        grid=(nbatch // bb, nt, len(POOL_WINDOWS)),
        in_specs=in_specs,
        out_specs=pl.BlockSpec((ts * bb, tc), lambda b, t, g: (b * nt + t, g)),
        scratch_shapes=[pltpu.VMEM((HALO + ts, tc), F32)],
        compiler_params=_cparams(("parallel", "arbitrary", "arbitrary")),
        name="pool",
    )(*args)


def _mlstm_body(*refs, L, LP, HB, has_state, has_alias):
    refs = list(refs)
    (q_ref, k_ref, v_ref, zo_ref, zgm_ref, zic_ref, zfc_ref, zir_ref, zfr_ref, hn_ref) = refs[:10]
    refs = refs[10:]
    if has_state:
        c0_ref, n0_ref, m0_ref = refs[:3]
        refs = refs[3:]
    if has_alias:
        refs = refs[1:]
    h_ref, c_ref, n_ref, m_ref = refs
    hb = pl.program_id(1)
    c = pl.program_id(2)

    @pl.when(c == 0)
    def _():
        if has_state:
            c_ref[...] = c0_ref[...]
            n_ref[...] = n0_ref[...]
            m_ref[...] = m0_ref[...]
        else:
            c_ref[...] = jnp.zeros_like(c_ref)
            n_ref[...] = jnp.zeros_like(n_ref)
            m_ref[...] = jnp.zeros_like(m_ref)

    t_ids = lax.broadcasted_iota(jnp.int32, (LP, LP), 0)
    s_ids = lax.broadcasted_iota(jnp.int32, (LP, LP), 1)
    tri = s_ids <= t_ids
    tri_t = t_ids <= s_ids
    head_lane = lax.broadcasted_iota(jnp.int32, (L, N_HEADS), 1)
    head_sub = lax.broadcasted_iota(jnp.int32, (N_HEADS, LP), 0)
    zi_c = zic_ref[...]
    zf_c = zfc_ref[...]
    zi_r = zir_ref[0]
    zf_r = zfr_ref[0]
    pad = LP - L

    def pad_rows(x, value):
        if pad == 0:
            return x
        return jnp.concatenate([x, jnp.full((pad, x.shape[1]), value, x.dtype)], axis=0)

    for j in range(HB):
        head = hb * HB + j
        logi_c = pad_rows(jnp.sum(jnp.where(head_lane == head, zi_c, 0.0), axis=1, keepdims=True), -jnp.inf)
        logf_c = pad_rows(_log_sigmoid(jnp.sum(jnp.where(head_lane == head, zf_c, 0.0), axis=1, keepdims=True)), 0.0)
        logi_r = jnp.sum(jnp.where(head_sub == head, zi_r, 0.0), axis=0, keepdims=True)
        logf_r = _log_sigmoid(jnp.sum(jnp.where(head_sub == head, zf_r, 0.0), axis=0, keepdims=True))

        b_c = jnp.sum(jnp.where(tri, logf_r, 0.0), axis=1, keepdims=True)
        b_r = jnp.sum(jnp.where(tri_t, logf_c, 0.0), axis=0, keepdims=True)
        b_last = jnp.sum(logf_r, axis=1, keepdims=True)
        a_c = logi_c - b_c
        a_r = logi_r - b_r
        m_prev = m_ref[0, j]
        big_m = jnp.maximum(m_prev, jnp.max(jnp.where(tri, a_r, -jnp.inf), axis=1, keepdims=True))
        m_last = jnp.maximum(m_prev, jnp.max(a_r, axis=1, keepdims=True))
        d = jnp.exp(jnp.where(tri, a_r - big_m, -jnp.inf))
        inter = jnp.exp(m_prev - big_m)

        q = pad_rows(q_ref[:, j * DK:(j + 1) * DK], 0.0).astype(BF16)
        k = pad_rows(k_ref[:, j * DK:(j + 1) * DK], 0.0)
        v = pad_rows(v_ref[:, j * DV:(j + 1) * DV], 0.0).astype(BF16)
        cmat = c_ref[0, j]
        n_row = n_ref[0, j]

        qk = lax.dot_general(q, k.astype(BF16), (((1,), (1,)), ((), ())), preferred_element_type=F32)
        s = qk * (d * (DK ** -0.5))
        num = inter * jnp.dot(q, cmat.astype(BF16), preferred_element_type=F32)
        num = num + jnp.dot(s.astype(BF16), v, preferred_element_type=F32)
        qn = jnp.sum(q.astype(F32) * n_row, axis=1, keepdims=True)
        den = inter * qn + jnp.sum(s, axis=1, keepdims=True)
        m_t = b_c + big_m
        hval = num * (1.0 / jnp.maximum(jnp.abs(den), jnp.exp(-m_t)))

        w_end = jnp.exp(a_c - m_last)
        decay = jnp.exp(m_prev - m_last)
        kw = k.astype(F32) * (w_end * (DK ** -0.5))
        kv = lax.dot_general(kw.astype(BF16), v, (((0,), (0,)), ((), ())), preferred_element_type=F32)
        c_ref[0, j] = decay * cmat + kv
        n_ref[0, j] = decay * n_row + jnp.sum(kw, axis=0, keepdims=True)
        m_ref[0, j] = b_last + m_last

        hval = hval[:L]
        hnorm = hval * lax.rsqrt(jnp.mean(hval * hval, axis=-1, keepdims=True) + EPS)
        hnorm = hnorm * hn_ref[:, j * DV:(j + 1) * DV]
        hm = hnorm * _sigmoid(zo_ref[:, j * DV:(j + 1) * DV])
        h_ref[:, j * DV:(j + 1) * DV] = _sigmoid(zgm_ref[:, j * DV:(j + 1) * DV]) * hm


def _mlstm(qkv, z_o, z_r, zi_col, zf_col, zi_row, zf_row, head_norm, *, nbatch, seq, L, LP, HB, row0,
           state=None, c_stack=None):
    nc = seq // L
    nhb = N_HEADS // HB
    zrow0 = row0 // L
    in_specs = [
        pl.BlockSpec((L, HB * DK), lambda b, h, c: (b * nc + c, h)),
        pl.BlockSpec((L, HB * DK), lambda b, h, c: (b * nc + c, nhb + h)),
        pl.BlockSpec((L, HB * DV), lambda b, h, c: (b * nc + c, nhb + h)),
        pl.BlockSpec((L, HB * DV), lambda b, h, c: (zrow0 + b * nc + c, h)),
        pl.BlockSpec((L, HB * DV), lambda b, h, c: (zrow0 + b * nc + c, nhb + h)),
        pl.BlockSpec((L, N_HEADS), lambda b, h, c: (b * nc + c, 0)),
        pl.BlockSpec((L, N_HEADS), lambda b, h, c: (b * nc + c, 0)),
        pl.BlockSpec((1, N_HEADS, LP), lambda b, h, c: (b, 0, c)),
        pl.BlockSpec((1, N_HEADS, LP), lambda b, h, c: (b, 0, c)),
        pl.BlockSpec((1, HB * DV), lambda b, h, c: (0, h)),
    ]
    args = [qkv, qkv, qkv, z_o, z_r, zi_col, zf_col, zi_row, zf_row, head_norm]
    aliases = {}
    if state is not None:
        c0, n0, m0, layer = state
        in_specs += [
            pl.BlockSpec((None, 1, HB, DK, DV), lambda b, h, c: (layer, b, h, 0, 0)),
            pl.BlockSpec((None, 1, HB, 1, DK), lambda b, h, c: (layer, b, h, 0, 0)),
            pl.BlockSpec((None, 1, HB, 1, 1), lambda b, h, c: (layer, b, h, 0, 0)),
        ]
        args += [c0, n0, m0]
        c_shape = jax.ShapeDtypeStruct((DEPTH, nbatch, N_HEADS, DK, DV), F32)
        c_spec = pl.BlockSpec((None, 1, HB, DK, DV), lambda b, h, c: (layer, b, h, 0, 0))
        if c_stack is not None:
            in_specs.append(pl.BlockSpec(memory_space=pl.ANY))
            args.append(c_stack)
            aliases = {len(args) - 1: 1}
    else:
        c_shape = jax.ShapeDtypeStruct((nbatch, N_HEADS, DK, DV), F32)
        c_spec = pl.BlockSpec((1, HB, DK, DV), lambda b, h, c: (b, h, 0, 0))
    out_shape = (
        jax.ShapeDtypeStruct((nbatch * seq, N_HEADS * DV), F32),
        c_shape,
        jax.ShapeDtypeStruct((nbatch, N_HEADS, 1, DK), F32),
        jax.ShapeDtypeStruct((nbatch, N_HEADS, 1, 1), F32),
    )
    out_specs = (
        pl.BlockSpec((L, HB * DV), lambda b, h, c: (b * nc + c, h)),
        c_spec,
        pl.BlockSpec((1, HB, 1, DK), lambda b, h, c: (b, h, 0, 0)),
        pl.BlockSpec((1, HB, 1, 1), lambda b, h, c: (b, h, 0, 0)),
    )
    return pl.pallas_call(
        functools.partial(_mlstm_body, L=L, LP=LP, HB=HB, has_state=state is not None,
                          has_alias=c_stack is not None),
        out_shape=out_shape,
        grid=(nbatch, nhb, nc),
        in_specs=in_specs,
        out_specs=out_specs,
        input_output_aliases=aliases,
        compiler_params=_cparams(("parallel", "parallel", "arbitrary")),
        name="mlstm_state" if state is not None else "mlstm",
    )(*args)


def _pad_zeros(w, axis, n):
    shape = list(w.shape)
    shape[axis] = n
    return jnp.concatenate([w, jnp.zeros(shape, w.dtype)], axis=axis)


def _time_on_lanes(z, nbatch, seq, lp, fill):
    zr = jnp.transpose(z.reshape(nbatch, seq, N_HEADS), (0, 2, 1))
    if seq < lp:
        zr = jnp.pad(zr, ((0, 0), (0, 0), (0, lp - seq)), constant_values=fill)
    return zr


@jax.jit
def _forward(x_prompt, x_sample, state_c, state_n, state_m, state_pool, norm_ff1, w_ff1_in, w_ff1_out,
             norm_mix, w_in, b_in, head_norm, w_pool, pool_scale, w_out, norm_ff2, w_ff2_in, w_ff2_out,
             norm_final):
    pad_ff = D_FF_PAD - D_FF
    wu1 = _pad_zeros(w_ff1_in[:, :, D_FF:].astype(BF16), 2, pad_ff)
    wu2 = _pad_zeros(w_ff2_in[:, :, D_FF:].astype(BF16), 2, pad_ff)
    wo1 = _pad_zeros(w_ff1_out.astype(BF16), 1, pad_ff)
    wo2 = _pad_zeros(w_ff2_out.astype(BF16), 1, pad_ff)
    w_rest = w_in[:, :, COL_REST:].astype(BF16)
    b_all = b_in.reshape(DEPTH, 1, b_in.shape[1])
    b_rest = b_all[:, :, COL_REST:]
    pool_scale3 = pool_scale.reshape(DEPTH, 1, D_MODEL)

    state_n5 = state_n.reshape(DEPTH, DEC_BATCH, N_HEADS, 1, DK)
    state_m5 = state_m.reshape(DEPTH, DEC_BATCH, N_HEADS, 1, 1)

    x = jnp.concatenate([x_prompt.reshape(T_PROMPT, D_MODEL), x_sample.reshape(T_SAMPLE, D_MODEL)], axis=0)
    cs_p, ns_p, ms_p, ps_p, ns_s, ms_s, ps_s = ([] for _ in range(7))
    c_stack = None
    for l in range(DEPTH):
        a = _ffn_in(_rmsnorm(x, norm_ff1[l], BF16), w_ff1_in, wu1, l)
        x = _ffn_out(a, wo1, l, x)

        hn = _rmsnorm(x, norm_mix[l], BF16)
        qkv = _mm_bias(hn, w_in, b_all, l, COL_QKV, N_QKV, BF16, 512)
        z_o = _mm_bias(hn, w_in, b_all, l, COL_O, D_MODEL, F32, 512)
        gates = _mm_bias(hn, w_in, b_all, l, COL_IF, LANES, F32, LANES)
        z_r = _mm_bias(hn, w_rest, b_rest, l, 0, N_REST, F32, 1024)
        zi, zf = gates[:, :N_HEADS], gates[:, N_HEADS:2 * N_HEADS]
        hnorm_w = head_norm[l].reshape(1, N_HEADS * DV)

        hmg_p, c_p, n_p, m_p = _mlstm(
            qkv, z_o, z_r, zi[:T_PROMPT], zf[:T_PROMPT],
            _time_on_lanes(zi[:T_PROMPT], BATCH, SEQ, MLSTM_CHUNK, 0.0),
            _time_on_lanes(zf[:T_PROMPT], BATCH, SEQ, MLSTM_CHUNK, 0.0),
            hnorm_w, nbatch=BATCH, seq=SEQ, L=MLSTM_CHUNK, LP=MLSTM_CHUNK, HB=1, row0=0)
        hmg_s, c_stack, n_s, m_s = _mlstm(
            qkv[T_PROMPT:].astype(F32), z_o, z_r, zi[T_PROMPT:], zf[T_PROMPT:],
            _time_on_lanes(zi[T_PROMPT:], DEC_BATCH, DEC_SEQ, LANES, -jnp.inf),
            _time_on_lanes(zf[T_PROMPT:], DEC_BATCH, DEC_SEQ, LANES, jnp.inf),
            hnorm_w, nbatch=DEC_BATCH, seq=DEC_SEQ, L=DEC_SEQ, LP=LANES, HB=N_HEADS, row0=T_PROMPT,
            state=(state_c, state_n5, state_m5, l), c_stack=c_stack)

        pooled_p = _pool(z_r, 0, BATCH, SEQ, 256, 1, 0)
        pooled_s = _pool(z_r, T_PROMPT, DEC_BATCH, DEC_SEQ, DEC_SEQ, 16, PAST_LEN, prefix=state_pool, layer=l)

        merged = _pool_merge(pooled_p, pooled_s, w_pool, l, pool_scale3, z_r, hmg_p, hmg_s)
        x = _mm_resid(merged, w_out, l, x)

        a = _ffn_in(_rmsnorm(x, norm_ff2[l], BF16), w_ff2_in, wu2, l)
        x = _ffn_out(a, wo2, l, x)

        cs_p.append(c_p)
        ns_p.append(n_p.reshape(BATCH, N_HEADS, DK))
        ms_p.append(m_p.reshape(BATCH, N_HEADS))
        ps_p.append(jnp.stack([z_r[(b + 1) * SEQ - POOL_BUF:(b + 1) * SEQ, :D_MODEL] for b in range(BATCH)]))
        ns_s.append(n_s.reshape(DEC_BATCH, N_HEADS, DK))
        ms_s.append(m_s.reshape(DEC_BATCH, N_HEADS))
        zu_s = z_r[T_PROMPT:, :D_MODEL].reshape(DEC_BATCH, DEC_SEQ, D_MODEL)
        ps_s.append(jnp.concatenate([state_pool[l][:, DEC_SEQ:], zu_s], axis=1))

    y = _rmsnorm(x, norm_final, F32)
    return (y[:T_PROMPT].reshape(BATCH, SEQ, D_MODEL), y[T_PROMPT:].reshape(DEC_BATCH, DEC_SEQ, D_MODEL),
            jnp.stack(cs_p), jnp.stack(ns_p), jnp.stack(ms_p), jnp.stack(ps_p),
            c_stack, jnp.stack(ns_s), jnp.stack(ms_s), jnp.stack(ps_s))


def kernel(x_prompt, x_sample, state_mlstm_C, state_mlstm_n, state_mlstm_m, state_pool, norm_ff1, w_ff1_in,
           w_ff1_out, norm_mix, w_in, b_in, head_norm, w_pool, pool_scale, w_out, norm_ff2, w_ff2_in, w_ff2_out,
           norm_final):
    return _forward(x_prompt, x_sample, state_mlstm_C, state_mlstm_n, state_mlstm_m, state_pool, norm_ff1,
                    w_ff1_in, w_ff1_out, norm_mix, w_in, b_in, head_norm, w_pool, pool_scale, w_out, norm_ff2,
                    w_ff2_in, w_ff2_out, norm_final)
```

```python
import functools
import math

import jax
import jax.numpy as jnp
from jax import lax
from jax.experimental import pallas as pl
from jax.experimental.pallas import tpu as pltpu

F32 = jnp.float32
BF16 = jnp.bfloat16

D_MODEL = 4096
BATCH = 4
SEQ = 2048
DEPTH = 4
DEC_BATCH = 128
DEC_SEQ = 8
PAST_LEN = 16384
N_HEADS = 8
DK = D_MODEL // (2 * N_HEADS)
DV = D_MODEL // N_HEADS
POOL_WINDOWS = (2, 4, 8, 16)
POOL_GW = D_MODEL // len(POOL_WINDOWS)
POOL_BUF = max(POOL_WINDOWS) - 1
D_FF = int(math.ceil(8 * D_MODEL / 3 / 64)) * 64
EPS = 1e-6

T_PROMPT = BATCH * SEQ
T_SAMPLE = DEC_BATCH * DEC_SEQ
T_ALL = T_PROMPT + T_SAMPLE

COL_QKV = 0
N_QKV = 2 * N_HEADS * DK + N_HEADS * DV
COL_O = N_QKV
COL_IF = COL_O + D_MODEL
COL_REST = COL_IF + 2 * N_HEADS
N_REST = 3 * D_MODEL

LANES = 128
SUBLANES = 8
MXU_WIDTH = 256
VMEM_LIMIT_BYTES = 56 * 1024 * 1024

D_FF_PAD = ((D_FF + MXU_WIDTH - 1) // MXU_WIDTH) * MXU_WIDTH
TM = 1024
TM_FFN_IN = 1536
MLSTM_CHUNK = 256
HALO = POOL_BUF + 1


def _cparams(semantics):
    return pltpu.CompilerParams(dimension_semantics=semantics, vmem_limit_bytes=VMEM_LIMIT_BYTES)


def _sigmoid(x):
    return 1.0 / (1.0 + jnp.exp(-x))


def _log_sigmoid(x):
    return jnp.minimum(x, 0.0) - jnp.log1p(jnp.exp(-jnp.abs(x)))


def _bf16(w):
    return w if w.dtype == BF16 else w.astype(BF16)


def _rmsnorm_body(x_ref, g_ref, o_ref):
    x = x_ref[...]
    y = x * lax.rsqrt(jnp.mean(x * x, axis=-1, keepdims=True) + EPS)
    o_ref[...] = (y * g_ref[...]).astype(o_ref.dtype)


def _rmsnorm(x, g, out_dtype, tr=256):
    t, d = x.shape
    return pl.pallas_call(
        _rmsnorm_body,
        out_shape=jax.ShapeDtypeStruct((t, d), out_dtype),
        grid=(t // tr,),
        in_specs=[pl.BlockSpec((tr, d), lambda i: (i, 0)),
                  pl.BlockSpec((1, d), lambda i: (0, 0))],
        out_specs=pl.BlockSpec((tr, d), lambda i: (i, 0)),
        compiler_params=_cparams(("parallel",)),
        name="rmsnorm",
    )(x, g.reshape(1, d))


def _in_proj_body(x_ref, w_ref, b_ref, o_ref):
    w = w_ref[0].astype(BF16)
    acc = lax.dot_general(x_ref[...], w, (((1,), (1,)), ((), ())), preferred_element_type=F32)
    o_ref[...] = (acc + b_ref[...]).astype(o_ref.dtype)


def _in_proj(x, w_t, bias, layer, row0, out_dtype, tn):
    t, k = x.shape
    ncols = bias.shape[2]
    assert row0 % SUBLANES == 0 and ncols % tn == 0
    return pl.pallas_call(
        _in_proj_body,
        out_shape=jax.ShapeDtypeStruct((t, ncols), out_dtype),
        grid=(t // TM, ncols // tn),
        in_specs=[pl.BlockSpec((TM, k), lambda i, j: (i, 0)),
                  pl.BlockSpec((pl.Element(1), pl.Element(tn), pl.Element(k)),
                               lambda i, j: (layer, pl.multiple_of(row0 + j * tn, SUBLANES), 0)),
                  pl.BlockSpec((None, 1, tn), lambda i, j: (layer, 0, j))],
        out_specs=pl.BlockSpec((TM, tn), lambda i, j: (i, j)),
        compiler_params=_cparams(("parallel", "arbitrary")),
        name="in_proj",
    )(x, w_t, bias)


def _mm_resid_body(x_ref, w_ref, r_ref, o_ref):
    acc = jnp.dot(x_ref[...], _bf16(w_ref[...]), preferred_element_type=F32)
    o_ref[...] = r_ref[...] + acc


def _mm_resid(x, w_stack, layer, resid, tn=512):
    t, k = x.shape
    n = w_stack.shape[2]
    return pl.pallas_call(
        _mm_resid_body,
        out_shape=jax.ShapeDtypeStruct((t, n), F32),
        grid=(t // TM, n // tn),
        in_specs=[pl.BlockSpec((TM, k), lambda i, j: (i, 0)),
                  pl.BlockSpec((None, k, tn), lambda i, j: (layer, 0, j)),
                  pl.BlockSpec((TM, tn), lambda i, j: (i, j))],
        out_specs=pl.BlockSpec((TM, tn), lambda i, j: (i, j)),
        compiler_params=_cparams(("parallel", "arbitrary")),
        name="out_proj",
    )(x, w_stack, resid)


def _ffn_in_body(x_ref, wg_ref, wu_ref, o_ref, *, tn):
    x = x_ref[...]
    g = jnp.dot(x, _bf16(wg_ref[...]), preferred_element_type=F32)
    u = jnp.dot(x, wu_ref[...], preferred_element_type=F32)
    col = pl.program_id(1) * tn + lax.broadcasted_iota(jnp.int32, g.shape, 1)
    o_ref[...] = jnp.where(col < D_FF, g * _sigmoid(g) * u, 0.0).astype(o_ref.dtype)


def _ffn_in(x, w_in_stack, wu_stack, layer, tn=MXU_WIDTH):
    t, k = x.shape
    return pl.pallas_call(
        functools.partial(_ffn_in_body, tn=tn),
        out_shape=jax.ShapeDtypeStruct((t, D_FF_PAD), BF16),
        grid=(t // TM_FFN_IN, D_FF_PAD // tn),
        in_specs=[pl.BlockSpec((TM_FFN_IN, k), lambda i, j: (i, 0)),
                  pl.BlockSpec((None, k, tn), lambda i, j: (layer, 0, j)),
                  pl.BlockSpec((None, k, tn), lambda i, j: (layer, 0, j))],
        out_specs=pl.BlockSpec((TM_FFN_IN, tn), lambda i, j: (i, j)),
        compiler_params=_cparams(("parallel", "arbitrary")),
        name="ffn_in",
    )(x, w_in_stack, wu_stack)


def _cast_pad_rows_body(w_ref, o_ref, *, tr, valid):
    rows = pl.program_id(1) * tr + lax.broadcasted_iota(jnp.int32, w_ref.shape, 0)
    o_ref[...] = jnp.where(rows < valid, w_ref[...], 0.0).astype(o_ref.dtype)


def _cast_pad_rows(w, rows_out, tr):
    depth, r, n = w.shape
    assert rows_out % tr == 0
    return pl.pallas_call(
        functools.partial(_cast_pad_rows_body, tr=tr, valid=r),
        out_shape=jax.ShapeDtypeStruct((depth, rows_out, n), BF16),
        grid=(depth, rows_out // tr),
        in_specs=[pl.BlockSpec((None, tr, n), lambda l, i: (l, i, 0))],
        out_specs=pl.BlockSpec((None, tr, n), lambda l, i: (l, i, 0)),
        compiler_params=_cparams(("parallel", "arbitrary")),
        name="cast_pad",
    )(w)


def _ffn_out_body(a_ref, w_ref, r_ref, o_ref, acc_ref, *, nk):
    k = pl.program_id(2)

    @pl.when(k == 0)
    def _():
        acc_ref[...] = jnp.zeros_like(acc_ref)

    acc_ref[...] += jnp.dot(a_ref[...], w_ref[...], preferred_element_type=F32)

    @pl.when(k == nk - 1)
    def _():
        o_ref[...] = r_ref[...] + 0.5 * acc_ref[...]


def _ffn_out(a, w_stack, layer, resid, tn=512, nk=2):
    t, kdim = a.shape
    n = w_stack.shape[2]
    tk = kdim // nk
    return pl.pallas_call(
        functools.partial(_ffn_out_body, nk=nk),
        out_shape=jax.ShapeDtypeStruct((t, n), F32),
        grid=(t // TM, n // tn, nk),
        in_specs=[pl.BlockSpec((TM, tk), lambda i, j, k: (i, k)),
                  pl.BlockSpec((None, tk, tn), lambda i, j, k: (layer, k, j)),
                  pl.BlockSpec((TM, tn), lambda i, j, k: (i, j))],
        out_specs=pl.BlockSpec((TM, tn), lambda i, j, k: (i, j)),
        scratch_shapes=[pltpu.VMEM((TM, tn), F32)],
        compiler_params=_cparams(("parallel", "arbitrary", "arbitrary")),
        name="ffn_out",
    )(a, w_stack, resid)


def _pool_merge_body(pp_ref, ps_ref, w_ref, sc_ref, zgp_ref, hp_ref, hs_ref, o_ref, *, n_prompt_tiles):
    i = pl.program_id(0)

    def merge(p_ref, h_ref):
        acc = jnp.dot(p_ref[...].astype(BF16), _bf16(w_ref[...]), preferred_element_type=F32)
        o_ref[...] = (h_ref[...] + _sigmoid(zgp_ref[...]) * (acc * sc_ref[...])).astype(o_ref.dtype)

    @pl.when(i < n_prompt_tiles)
    def _():
        merge(pp_ref, hp_ref)

    @pl.when(i >= n_prompt_tiles)
    def _():
        merge(ps_ref, hs_ref)


def _pool_merge(pooled_p, pooled_s, w_pool, layer, pool_scale, z_r, hmg_p, hmg_s):
    ngroups = len(POOL_WINDOWS)
    npt = T_PROMPT // TM
    assert T_SAMPLE == TM
    zgp_cb0 = 2 * D_MODEL // POOL_GW
    prompt_map = lambda i, g: (jnp.minimum(i, npt - 1), g)
    sample_map = lambda i, g: (0, jnp.where(i >= npt, g, 0))
    return pl.pallas_call(
        functools.partial(_pool_merge_body, n_prompt_tiles=npt),
        out_shape=jax.ShapeDtypeStruct((T_ALL, D_MODEL), BF16),
        grid=(T_ALL // TM, ngroups),
        in_specs=[pl.BlockSpec((TM, POOL_GW), prompt_map),
                  pl.BlockSpec((TM, POOL_GW), sample_map),
                  pl.BlockSpec((None, None, POOL_GW, POOL_GW), lambda i, g: (layer, g, 0, 0)),
                  pl.BlockSpec((None, 1, POOL_GW), lambda i, g: (layer, 0, g)),
                  pl.BlockSpec((TM, POOL_GW), lambda i, g: (i, zgp_cb0 + g)),
                  pl.BlockSpec((TM, POOL_GW), prompt_map),
                  pl.BlockSpec((TM, POOL_GW), sample_map)],
        out_specs=pl.BlockSpec((TM, POOL_GW), lambda i, g: (i, g)),
        compiler_params=_cparams(("parallel", "arbitrary")),
        name="pool_merge",
    )(pooled_p, pooled_s, w_pool, pool_scale, z_r, hmg_p, hmg_s)


def _pool_body(*refs, ts, bb, pos0, use_prev, has_prefix):
    refs = list(refs)
    cur_ref = refs.pop(0)
    prev_ref = refs.pop(0) if use_prev else None
    pre_ref = refs.pop(0) if has_prefix else None
    o_ref, ext_ref = refs
    t = pl.program_id(1)
    g = pl.program_id(2)
    rows = lax.broadcasted_iota(jnp.int32, (ts, 1), 0)
    lead = HALO - POOL_BUF
    for i in range(bb):
        def first_tile_halo():
            if has_prefix:
                ext_ref[lead:HALO, :] = pre_ref[i]
            else:
                ext_ref[0:HALO, :] = jnp.zeros((HALO, ext_ref.shape[1]), F32)

        if use_prev:
            pl.when(t == 0)(first_tile_halo)

            @pl.when(t > 0)
            def _():
                ext_ref[0:HALO, :] = prev_ref[...]
        else:
            first_tile_halo()
        ext_ref[HALO:HALO + ts, :] = cur_ref[i * ts:(i + 1) * ts, :]

        for gi, w in enumerate(POOL_WINDOWS):
            @pl.when(g == gi)
            def _():
                cur = ext_ref[pl.ds(HALO, ts), :]
                acc = cur
                for j in range(1, w):
                    acc = acc + ext_ref[pl.ds(HALO - j, ts), :]
                cnt = jnp.minimum(pos0 + t * ts + rows + 1, w).astype(F32)
                o_ref[i * ts:(i + 1) * ts, :] = acc / cnt - cur


def _pool(z_r, row0, nbatch, seq, ts, bb, pos0, prefix=None, layer=0):
    nt = seq // ts
    use_prev = nt > 1
    tc = POOL_GW
    rb0 = row0 // (ts * bb)
    in_specs = [pl.BlockSpec((ts * bb, tc), lambda b, t, g: (rb0 + b * nt + t, g))]
    args = [z_r]
    if use_prev:
        per = ts // HALO
        in_specs.append(pl.BlockSpec(
            (HALO, tc), lambda b, t, g: (jnp.maximum((rb0 + b * nt + t) * per - 1, 0), g)))
        args.append(z_r)
    if prefix is not None:
        in_specs.append(pl.BlockSpec((None, bb, POOL_BUF, tc), lambda b, t, g: (layer, b, 0, g)))
        args.append(prefix)
    return pl.pallas_call(
        functools.partial(_pool_body, ts=ts, bb=bb, pos0=pos0, use_prev=use_prev, has_prefix=prefix is not None),
        out_shape=jax.ShapeDtypeStruct((nbatch * seq, D_MODEL), F32),
        grid=(nbatch // bb, nt, len(POOL_WINDOWS)),
        in_specs=in_specs,
        out_specs=pl.BlockSpec((ts * bb, tc), lambda b, t, g: (b * nt + t, g)),
        scratch_shapes=[pltpu.VMEM((HALO + ts, tc), F32)],
        compiler_params=_cparams(("parallel", "arbitrary", "arbitrary")),
        name="pool",
    )(*args)


def _mlstm_body(*refs, L, LP, HB, has_state, has_alias):
    refs = list(refs)
    (q_ref, k_ref, v_ref, zo_ref, zgm_ref, zic_ref, zfc_ref, zir_ref, zfr_ref, hn_ref) = refs[:10]
    refs = refs[10:]
    if has_state:
        c0_ref, n0_ref, m0_ref = refs[:3]
        refs = refs[3:]
    if has_alias:
        refs = refs[1:]
    h_ref, c_ref, n_ref, m_ref = refs
    hb = pl.program_id(1)
    c = pl.program_id(2)

    @pl.when(c == 0)
    def _():
        if has_state:
            c_ref[...] = c0_ref[...]
            n_ref[...] = n0_ref[...]
            m_ref[...] = m0_ref[...]
        else:
            c_ref[...] = jnp.zeros_like(c_ref)
            n_ref[...] = jnp.zeros_like(n_ref)
            m_ref[...] = jnp.zeros_like(m_ref)

    row_ids = lax.broadcasted_iota(jnp.int32, (L, LP), 0)
    lane_ids = lax.broadcasted_iota(jnp.int32, (L, LP), 1)
    tri = lane_ids <= row_ids
    tri_t = row_ids <= lane_ids
    head_lane = lax.broadcasted_iota(jnp.int32, (L, N_HEADS), 1)
    head_sub = lax.broadcasted_iota(jnp.int32, (N_HEADS, LP), 0)
    zi_c = zic_ref[...]
    zf_c = zfc_ref[...]
    zi_r = zir_ref[0]
    zf_r = zfr_ref[0]
    pad = LP - L

    def pad_rows(x):
        if pad == 0:
            return x
        return jnp.concatenate([x, jnp.zeros((pad, x.shape[1]), x.dtype)], axis=0)

    for j in range(HB):
        head = hb * HB + j
        logi_c = jnp.sum(jnp.where(head_lane == head, zi_c, 0.0), axis=1, keepdims=True)
        logf_c = _log_sigmoid(jnp.sum(jnp.where(head_lane == head, zf_c, 0.0), axis=1, keepdims=True))
        logi_r = jnp.sum(jnp.where(head_sub == head, zi_r, 0.0), axis=0, keepdims=True)
        logf_r = _log_sigmoid(jnp.sum(jnp.where(head_sub == head, zf_r, 0.0), axis=0, keepdims=True))

        b_c = jnp.sum(jnp.where(tri, logf_r, 0.0), axis=1, keepdims=True)
        b_r = jnp.sum(jnp.where(tri_t, logf_c, 0.0), axis=0, keepdims=True)
        b_last = jnp.sum(logf_r, axis=1, keepdims=True)
        a_c = logi_c - b_c
        a_r = logi_r - b_r
        m_prev = m_ref[0, j]
        big_m = jnp.maximum(m_prev, jnp.max(jnp.where(tri, a_r, -jnp.inf), axis=1, keepdims=True))
        m_last = jnp.maximum(m_prev, jnp.max(a_r, axis=1, keepdims=True))
        d = jnp.exp(jnp.where(tri, a_r - big_m, -jnp.inf))
        inter = jnp.exp(m_prev - big_m)

        q = q_ref[:, j * DK:(j + 1) * DK].astype(BF16)
        k = k_ref[:, j * DK:(j + 1) * DK]
        v = pad_rows(v_ref[:, j * DV:(j + 1) * DV]).astype(BF16)
        cmat = c_ref[0, j]
        n_row = n_ref[0, j]

        qk = lax.dot_general(q, pad_rows(k).astype(BF16), (((1,), (1,)), ((), ())), preferred_element_type=F32)
        s = qk * (d * (DK ** -0.5))
        num = inter * jnp.dot(q, cmat.astype(BF16), preferred_element_type=F32)
        num = num + jnp.dot(s.astype(BF16), v, preferred_element_type=F32)
        qn = jnp.sum(q.astype(F32) * n_row, axis=1, keepdims=True)
        den = inter * qn + jnp.sum(s, axis=1, keepdims=True)
        m_t = b_c + big_m
        hval = num * (1.0 / jnp.maximum(jnp.abs(den), jnp.exp(-m_t)))

        w_end = jnp.exp(a_c - m_last)
        decay = jnp.exp(m_prev - m_last)
        kw = k.astype(F32) * (w_end * (DK ** -0.5))
        kv = lax.dot_general(pad_rows(kw).astype(BF16), v, (((0,), (0,)), ((), ())), preferred_element_type=F32)
        c_ref[0, j] = decay * cmat + kv
        n_ref[0, j] = decay * n_row + jnp.sum(kw, axis=0, keepdims=True)
        m_ref[0, j] = b_last + m_last

        hnorm = hval * lax.rsqrt(jnp.mean(hval * hval, axis=-1, keepdims=True) + EPS)
        hnorm = hnorm * hn_ref[:, j * DV:(j + 1) * DV]
        hm = hnorm * _sigmoid(zo_ref[:, j * DV:(j + 1) * DV])
        h_ref[:, j * DV:(j + 1) * DV] = _sigmoid(zgm_ref[:, j * DV:(j + 1) * DV]) * hm


def _mlstm(qkv, z_o, z_r, zi_col, zf_col, zi_row, zf_row, head_norm, *, nbatch, seq, L, LP, HB, row0,
           state=None, c_stack=None):
    nc = seq // L
    nhb = N_HEADS // HB
    zrow0 = row0 // L
    in_specs = [
        pl.BlockSpec((L, HB * DK), lambda b, h, c: (b * nc + c, h)),
        pl.BlockSpec((L, HB * DK), lambda b, h, c: (b * nc + c, nhb + h)),
        pl.BlockSpec((L, HB * DV), lambda b, h, c: (b * nc + c, nhb + h)),
        pl.BlockSpec((L, HB * DV), lambda b, h, c: (zrow0 + b * nc + c, h)),
        pl.BlockSpec((L, HB * DV), lambda b, h, c: (zrow0 + b * nc + c, nhb + h)),
        pl.BlockSpec((L, N_HEADS), lambda b, h, c: (b * nc + c, 0)),
        pl.BlockSpec((L, N_HEADS), lambda b, h, c: (b * nc + c, 0)),
        pl.BlockSpec((1, N_HEADS, LP), lambda b, h, c: (b, 0, c)),
        pl.BlockSpec((1, N_HEADS, LP), lambda b, h, c: (b, 0, c)),
        pl.BlockSpec((1, HB * DV), lambda b, h, c: (0, h)),
    ]
    args = [qkv, qkv, qkv, z_o, z_r, zi_col, zf_col, zi_row, zf_row, head_norm]
    aliases = {}
    if state is not None:
        c0, n0, m0, layer = state
        in_specs += [
            pl.BlockSpec((None, 1, HB, DK, DV), lambda b, h, c: (layer, b, h, 0, 0)),
            pl.BlockSpec((None, 1, HB, 1, DK), lambda b, h, c: (layer, b, h, 0, 0)),
            pl.BlockSpec((None, 1, HB, 1, 1), lambda b, h, c: (layer, b, h, 0, 0)),
        ]
        args += [c0, n0, m0]
        c_shape = jax.ShapeDtypeStruct((DEPTH, nbatch, N_HEADS, DK, DV), F32)
        c_spec = pl.BlockSpec((None, 1, HB, DK, DV), lambda b, h, c: (layer, b, h, 0, 0))
        if c_stack is not None:
            in_specs.append(pl.BlockSpec(memory_space=pl.ANY))
            args.append(c_stack)
            aliases = {len(args) - 1: 1}
    else:
        c_shape = jax.ShapeDtypeStruct((nbatch, N_HEADS, DK, DV), F32)
        c_spec = pl.BlockSpec((1, HB, DK, DV), lambda b, h, c: (b, h, 0, 0))
    out_shape = (
        jax.ShapeDtypeStruct((nbatch * seq, N_HEADS * DV), F32),
        c_shape,
        jax.ShapeDtypeStruct((nbatch, N_HEADS, 1, DK), F32),
        jax.ShapeDtypeStruct((nbatch, N_HEADS, 1, 1), F32),
    )
    out_specs = (
        pl.BlockSpec((L, HB * DV), lambda b, h, c: (b * nc + c, h)),
        c_spec,
        pl.BlockSpec((1, HB, 1, DK), lambda b, h, c: (b, h, 0, 0)),
        pl.BlockSpec((1, HB, 1, 1), lambda b, h, c: (b, h, 0, 0)),
    )
    return pl.pallas_call(
        functools.partial(_mlstm_body, L=L, LP=LP, HB=HB, has_state=state is not None,
                          has_alias=c_stack is not None),
        out_shape=out_shape,
        grid=(nbatch, nhb, nc),
        in_specs=in_specs,
        out_specs=out_specs,
        input_output_aliases=aliases,
        compiler_params=_cparams(("parallel", "parallel", "arbitrary")),
        name="mlstm_state" if state is not None else "mlstm",
    )(*args)


def _time_on_lanes(z, nbatch, seq, lp, fill):
    zr = jnp.transpose(z.reshape(nbatch, seq, N_HEADS), (0, 2, 1))
    if seq < lp:
        zr = jnp.pad(zr, ((0, 0), (0, 0), (0, lp - seq)), constant_values=fill)
    return zr


@jax.jit
def _forward(x_prompt, x_sample, state_c, state_n, state_m, state_pool, norm_ff1, w_ff1_in, w_ff1_out,
             norm_mix, w_in, b_in, head_norm, w_pool, pool_scale, w_out, norm_ff2, w_ff2_in, w_ff2_out,
             norm_final):
    wu1 = w_ff1_in[:, :, D_FF:].astype(BF16)
    wu2 = w_ff2_in[:, :, D_FF:].astype(BF16)
    wo1 = _cast_pad_rows(w_ff1_out, D_FF_PAD, D_FF_PAD // 16)
    wo2 = _cast_pad_rows(w_ff2_out, D_FF_PAD, D_FF_PAD // 16)
    w_in_t = jnp.transpose(w_in, (0, 2, 1))
    b_all = b_in.reshape(DEPTH, 1, b_in.shape[1])
    b_qkv = b_all[:, :, COL_QKV:COL_QKV + N_QKV]
    b_o = b_all[:, :, COL_O:COL_O + D_MODEL]
    b_if = b_all[:, :, COL_IF:COL_IF + LANES]
    b_rest = b_all[:, :, COL_REST:]
    pool_scale3 = pool_scale.reshape(DEPTH, 1, D_MODEL)

    state_n5 = state_n.reshape(DEPTH, DEC_BATCH, N_HEADS, 1, DK)
    state_m5 = state_m.reshape(DEPTH, DEC_BATCH, N_HEADS, 1, 1)

    x = jnp.concatenate([x_prompt.reshape(T_PROMPT, D_MODEL), x_sample.reshape(T_SAMPLE, D_MODEL)], axis=0)
    cs_p, ns_p, ms_p, ps_p, ns_s, ms_s, ps_s = ([] for _ in range(7))
    c_stack = None
    for l in range(DEPTH):
        a = _ffn_in(_rmsnorm(x, norm_ff1[l], BF16), w_ff1_in, wu1, l)
        x = _ffn_out(a, wo1, l, x)

        hn = _rmsnorm(x, norm_mix[l], BF16)
        qkv = _in_proj(hn, w_in_t, b_qkv, l, COL_QKV, BF16, 512)
        z_o = _in_proj(hn, w_in_t, b_o, l, COL_O, F32, 512)
        gates = _in_proj(hn, w_in_t, b_if, l, COL_IF, F32, LANES)
        z_r = _in_proj(hn, w_in_t, b_rest, l, COL_REST, F32, 512)
        zi, zf = gates[:, :N_HEADS], gates[:, N_HEADS:2 * N_HEADS]
        hnorm_w = head_norm[l].reshape(1, N_HEADS * DV)

        hmg_p, c_p, n_p, m_p = _mlstm(
            qkv, z_o, z_r, zi[:T_PROMPT], zf[:T_PROMPT],
            _time_on_lanes(zi[:T_PROMPT], BATCH, SEQ, MLSTM_CHUNK, 0.0),
            _time_on_lanes(zf[:T_PROMPT], BATCH, SEQ, MLSTM_CHUNK, 0.0),
            hnorm_w, nbatch=BATCH, seq=SEQ, L=MLSTM_CHUNK, LP=MLSTM_CHUNK, HB=1, row0=0)
        hmg_s, c_stack, n_s, m_s = _mlstm(
            qkv[T_PROMPT:].astype(F32), z_o, z_r, zi[T_PROMPT:], zf[T_PROMPT:],
            _time_on_lanes(zi[T_PROMPT:], DEC_BATCH, DEC_SEQ, LANES, -jnp.inf),
            _time_on_lanes(zf[T_PROMPT:], DEC_BATCH, DEC_SEQ, LANES, jnp.inf),
            hnorm_w, nbatch=DEC_BATCH, seq=DEC_SEQ, L=DEC_SEQ, LP=LANES, HB=N_HEADS, row0=T_PROMPT,
            state=(state_c, state_n5, state_m5, l), c_stack=c_stack)

        pooled_p = _pool(z_r, 0, BATCH, SEQ, 256, 1, 0)
        pooled_s = _pool(z_r, T_PROMPT, DEC_BATCH, DEC_SEQ, DEC_SEQ, 16, PAST_LEN, prefix=state_pool, layer=l)

        merged = _pool_merge(pooled_p, pooled_s, w_pool, l, pool_scale3, z_r, hmg_p, hmg_s)
        x = _mm_resid(merged, w_out, l, x)

        a = _ffn_in(_rmsnorm(x, norm_ff2[l], BF16), w_ff2_in, wu2, l)
        x = _ffn_out(a, wo2, l, x)

        cs_p.append(c_p)
        ns_p.append(n_p.reshape(BATCH, N_HEADS, DK))
        ms_p.append(m_p.reshape(BATCH, N_HEADS))
        ps_p.append(jnp.stack([z_r[(b + 1) * SEQ - POOL_BUF:(b + 1) * SEQ, :D_MODEL] for b in range(BATCH)]))
        ns_s.append(n_s.reshape(DEC_BATCH, N_HEADS, DK))
        ms_s.append(m_s.reshape(DEC_BATCH, N_HEADS))
        zu_s = z_r[T_PROMPT:, :D_MODEL].reshape(DEC_BATCH, DEC_SEQ, D_MODEL)
        ps_s.append(jnp.concatenate([state_pool[l][:, DEC_SEQ:], zu_s], axis=1))

    y = _rmsnorm(x, norm_final, F32)
    return (y[:T_PROMPT].reshape(BATCH, SEQ, D_MODEL), y[T_PROMPT:].reshape(DEC_BATCH, DEC_SEQ, D_MODEL),
            jnp.stack(cs_p), jnp.stack(ns_p), jnp.stack(ms_p), jnp.stack(ps_p),
            c_stack, jnp.stack(ns_s), jnp.stack(ms_s), jnp.stack(ps_s))


def kernel(x_prompt, x_sample, state_mlstm_C, state_mlstm_n, state_mlstm_m, state_pool, norm_ff1, w_ff1_in,
           w_ff1_out, norm_mix, w_in, b_in, head_norm, w_pool, pool_scale, w_out, norm_ff2, w_ff2_in, w_ff2_out,
           norm_final):
    return _forward(x_prompt, x_sample, state_mlstm_C, state_mlstm_n, state_mlstm_m, state_pool, norm_ff1,
                    w_ff1_in, w_ff1_out, norm_mix, w_in, b_in, head_norm, w_pool, pool_scale, w_out, norm_ff2,
                    w_ff2_in, w_ff2_out, norm_final)
```

```python
import functools
import math

import jax
import jax.numpy as jnp
from jax import lax
from jax.experimental import pallas as pl
from jax.experimental.pallas import tpu as pltpu

F32 = jnp.float32
BF16 = jnp.bfloat16

D_MODEL = 4096
BATCH = 4
SEQ = 2048
DEPTH = 4
DEC_BATCH = 128
DEC_SEQ = 8
PAST_LEN = 16384
N_HEADS = 8
DK = D_MODEL // (2 * N_HEADS)
DV = D_MODEL // N_HEADS
POOL_WINDOWS = (2, 4, 8, 16)
POOL_GW = D_MODEL // len(POOL_WINDOWS)
POOL_BUF = max(POOL_WINDOWS) - 1
D_FF = int(math.ceil(8 * D_MODEL / 3 / 64)) * 64
EPS = 1e-6

T_PROMPT = BATCH * SEQ
T_SAMPLE = DEC_BATCH * DEC_SEQ
T_ALL = T_PROMPT + T_SAMPLE

COL_QKV = 0
N_QKV = 2 * N_HEADS * DK + N_HEADS * DV
COL_O = N_QKV
COL_IF = COL_O + D_MODEL
COL_REST = COL_IF + 2 * N_HEADS
N_REST = 3 * D_MODEL

LANES = 128
SUBLANES = 8
MXU_WIDTH = 256
VMEM_LIMIT_BYTES = 56 * 1024 * 1024

D_FF_PAD = ((D_FF + MXU_WIDTH - 1) // MXU_WIDTH) * MXU_WIDTH
TM = 1024
TM_FFN_IN = 1536
MLSTM_CHUNK = 256
HALO = POOL_BUF + 1


def _cparams(semantics):
    return pltpu.CompilerParams(dimension_semantics=semantics, vmem_limit_bytes=VMEM_LIMIT_BYTES)


def _sigmoid(x):
    return 1.0 / (1.0 + jnp.exp(-x))


def _log_sigmoid(x):
    return jnp.minimum(x, 0.0) - jnp.log1p(jnp.exp(-jnp.abs(x)))


def _bf16(w):
    return w if w.dtype == BF16 else w.astype(BF16)


def _rmsnorm_body(x_ref, g_ref, o_ref):
    x = x_ref[...]
    y = x * lax.rsqrt(jnp.mean(x * x, axis=-1, keepdims=True) + EPS)
    o_ref[...] = (y * g_ref[...]).astype(o_ref.dtype)


def _rmsnorm(x, g, out_dtype, tr=256):
    t, d = x.shape
    return pl.pallas_call(
        _rmsnorm_body,
        out_shape=jax.ShapeDtypeStruct((t, d), out_dtype),
        grid=(t // tr,),
        in_specs=[pl.BlockSpec((tr, d), lambda i: (i, 0)),
                  pl.BlockSpec((1, d), lambda i: (0, 0))],
        out_specs=pl.BlockSpec((tr, d), lambda i: (i, 0)),
        compiler_params=_cparams(("parallel",)),
        name="rmsnorm",
    )(x, g.reshape(1, d))


def _rmsnorm_split_body(x_ref, g_ref, op_ref, os_ref, *, n_prompt_tiles):
    i = pl.program_id(0)
    x = x_ref[...]
    y = x * lax.rsqrt(jnp.mean(x * x, axis=-1, keepdims=True) + EPS) * g_ref[...]

    @pl.when(i < n_prompt_tiles)
    def _():
        op_ref[...] = y

    @pl.when(i >= n_prompt_tiles)
    def _():
        os_ref[...] = y


def _rmsnorm_split(x, g, tr=256):
    d = x.shape[1]
    npt = T_PROMPT // tr
    return pl.pallas_call(
        functools.partial(_rmsnorm_split_body, n_prompt_tiles=npt),
        out_shape=(jax.ShapeDtypeStruct((T_PROMPT, d), F32), jax.ShapeDtypeStruct((T_SAMPLE, d), F32)),
        grid=(T_ALL // tr,),
        in_specs=[pl.BlockSpec((tr, d), lambda i: (i, 0)),
                  pl.BlockSpec((1, d), lambda i: (0, 0))],
        out_specs=(pl.BlockSpec((tr, d), lambda i: (jnp.minimum(i, npt - 1), 0)),
                   pl.BlockSpec((tr, d), lambda i: (jnp.maximum(i - npt, 0), 0))),
        compiler_params=_cparams(("arbitrary",)),
        name="rmsnorm_out",
    )(x, g.reshape(1, d))


def _in_proj_body(x_ref, w_ref, b_ref, o_ref):
    w = w_ref[0].astype(BF16)
    acc = lax.dot_general(x_ref[...], w, (((1,), (1,)), ((), ())), preferred_element_type=F32)
    o_ref[...] = (acc + b_ref[...]).astype(o_ref.dtype)


def _in_proj(x, w_t, bias, layer, row0, out_dtype, tn):
    t, k = x.shape
    ncols = bias.shape[2]
    assert row0 % SUBLANES == 0 and ncols % tn == 0
    return pl.pallas_call(
        _in_proj_body,
        out_shape=jax.ShapeDtypeStruct((t, ncols), out_dtype),
        grid=(t // TM, ncols // tn),
        in_specs=[pl.BlockSpec((TM, k), lambda i, j: (i, 0)),
                  pl.BlockSpec((pl.Element(1), pl.Element(tn), pl.Element(k)),
                               lambda i, j: (layer, pl.multiple_of(row0 + j * tn, SUBLANES), 0)),
                  pl.BlockSpec((None, 1, tn), lambda i, j: (layer, 0, j))],
        out_specs=pl.BlockSpec((TM, tn), lambda i, j: (i, j)),
        compiler_params=_cparams(("parallel", "arbitrary")),
        name="in_proj",
    )(x, w_t, bias)


def _mm_resid_body(x_ref, w_ref, r_ref, o_ref, *, scale):
    acc = jnp.dot(x_ref[...], _bf16(w_ref[...]), preferred_element_type=F32)
    o_ref[...] = r_ref[...] + (acc if scale == 1.0 else scale * acc)


def _mm_resid(x, w_stack, layer, resid, *, scale=1.0, tm=TM, tn=512, name="out_proj"):
    t, k = x.shape
    n = w_stack.shape[2]
    return pl.pallas_call(
        functools.partial(_mm_resid_body, scale=scale),
        out_shape=jax.ShapeDtypeStruct((t, n), F32),
        grid=(t // tm, n // tn),
        in_specs=[pl.BlockSpec((tm, k), lambda i, j: (i, 0)),
                  pl.BlockSpec((None, k, tn), lambda i, j: (layer, 0, j)),
                  pl.BlockSpec((tm, tn), lambda i, j: (i, j))],
        out_specs=pl.BlockSpec((tm, tn), lambda i, j: (i, j)),
        compiler_params=_cparams(("parallel", "arbitrary")),
        name=name,
    )(x, w_stack, resid)


def _ffn_in_body(x_ref, wg_ref, wu_ref, o_ref, *, tn):
    x = x_ref[...]
    g = jnp.dot(x, _bf16(wg_ref[...]), preferred_element_type=F32)
    u = jnp.dot(x, wu_ref[...], preferred_element_type=F32)
    col = pl.program_id(1) * tn + lax.broadcasted_iota(jnp.int32, g.shape, 1)
    o_ref[...] = jnp.where(col < D_FF, g * _sigmoid(g) * u, 0.0).astype(o_ref.dtype)


def _ffn_in(x, w_in_stack, wu_stack, layer, tn=MXU_WIDTH):
    t, k = x.shape
    return pl.pallas_call(
        functools.partial(_ffn_in_body, tn=tn),
        out_shape=jax.ShapeDtypeStruct((t, D_FF_PAD), BF16),
        grid=(t // TM_FFN_IN, D_FF_PAD // tn),
        in_specs=[pl.BlockSpec((TM_FFN_IN, k), lambda i, j: (i, 0)),
                  pl.BlockSpec((None, k, tn), lambda i, j: (layer, 0, j)),
                  pl.BlockSpec((None, k, tn), lambda i, j: (layer, 0, j))],
        out_specs=pl.BlockSpec((TM_FFN_IN, tn), lambda i, j: (i, j)),
        compiler_params=_cparams(("parallel", "arbitrary")),
        name="ffn_in",
    )(x, w_in_stack, wu_stack)


def _cast_pad_rows_body(w_ref, o_ref, *, tr, valid):
    rows = pl.program_id(1) * tr + lax.broadcasted_iota(jnp.int32, w_ref.shape, 0)
    o_ref[...] = jnp.where(rows < valid, w_ref[...], 0.0).astype(o_ref.dtype)


def _cast_pad_rows(w, rows_out, tr):
    depth, r, n = w.shape
    assert rows_out % tr == 0
    return pl.pallas_call(
        functools.partial(_cast_pad_rows_body, tr=tr, valid=r),
        out_shape=jax.ShapeDtypeStruct((depth, rows_out, n), BF16),
        grid=(depth, rows_out // tr),
        in_specs=[pl.BlockSpec((None, tr, n), lambda l, i: (l, i, 0))],
        out_specs=pl.BlockSpec((None, tr, n), lambda l, i: (l, i, 0)),
        compiler_params=_cparams(("parallel", "arbitrary")),
        name="cast_pad",
    )(w)


def _ffn_out(a, w_stack, layer, resid):
    return _mm_resid(a, w_stack, layer, resid, scale=0.5, tm=TM // 2, tn=512, name="ffn_out")


def _pool_merge_body(pp_ref, ps_ref, w_ref, sc_ref, zgp_ref, hp_ref, hs_ref, o_ref, *, n_prompt_tiles):
    i = pl.program_id(0)

    def merge(p_ref, h_ref):
        acc = jnp.dot(p_ref[...].astype(BF16), _bf16(w_ref[...]), preferred_element_type=F32)
        o_ref[...] = (h_ref[...] + _sigmoid(zgp_ref[...]) * (acc * sc_ref[...])).astype(o_ref.dtype)

    @pl.when(i < n_prompt_tiles)
    def _():
        merge(pp_ref, hp_ref)

    @pl.when(i >= n_prompt_tiles)
    def _():
        merge(ps_ref, hs_ref)


def _pool_merge(pooled_p, pooled_s, w_pool, layer, pool_scale, z_r, hmg_p, hmg_s):
    ngroups = len(POOL_WINDOWS)
    npt = T_PROMPT // TM
    assert T_SAMPLE == TM
    zgp_cb0 = 2 * D_MODEL // POOL_GW
    prompt_map = lambda i, g: (jnp.minimum(i, npt - 1), g)
    sample_map = lambda i, g: (0, jnp.where(i >= npt, g, 0))
    return pl.pallas_call(
        functools.partial(_pool_merge_body, n_prompt_tiles=npt),
        out_shape=jax.ShapeDtypeStruct((T_ALL, D_MODEL), BF16),
        grid=(T_ALL // TM, ngroups),
        in_specs=[pl.BlockSpec((TM, POOL_GW), prompt_map),
                  pl.BlockSpec((TM, POOL_GW), sample_map),
                  pl.BlockSpec((None, None, POOL_GW, POOL_GW), lambda i, g: (layer, g, 0, 0)),
                  pl.BlockSpec((None, 1, POOL_GW), lambda i, g: (layer, 0, g)),
                  pl.BlockSpec((TM, POOL_GW), lambda i, g: (i, zgp_cb0 + g)),
                  pl.BlockSpec((TM, POOL_GW), prompt_map),
                  pl.BlockSpec((TM, POOL_GW), sample_map)],
        out_specs=pl.BlockSpec((TM, POOL_GW), lambda i, g: (i, g)),
        compiler_params=_cparams(("parallel", "arbitrary")),
        name="pool_merge",
    )(pooled_p, pooled_s, w_pool, pool_scale, z_r, hmg_p, hmg_s)


def _pool_body(*refs, ts, bb, pos0, use_prev, has_prefix):
    refs = list(refs)
    cur_ref = refs.pop(0)
    prev_ref = refs.pop(0) if use_prev else None
    pre_ref = refs.pop(0) if has_prefix else None
    o_ref, ext_ref = refs
    t = pl.program_id(1)
    g = pl.program_id(2)
    rows = lax.broadcasted_iota(jnp.int32, (ts, 1), 0)
    lead = HALO - POOL_BUF
    for i in range(bb):
        def first_tile_halo():
            if has_prefix:
                ext_ref[lead:HALO, :] = pre_ref[i]
            else:
                ext_ref[0:HALO, :] = jnp.zeros((HALO, ext_ref.shape[1]), F32)

        if use_prev:
            pl.when(t == 0)(first_tile_halo)

            @pl.when(t > 0)
            def _():
                ext_ref[0:HALO, :] = prev_ref[...]
        else:
            first_tile_halo()
        ext_ref[HALO:HALO + ts, :] = cur_ref[i * ts:(i + 1) * ts, :]

        for gi, w in enumerate(POOL_WINDOWS):
            @pl.when(g == gi)
            def _():
                cur = ext_ref[pl.ds(HALO, ts), :]
                acc = cur
                for j in range(1, w):
                    acc = acc + ext_ref[pl.ds(HALO - j, ts), :]
                cnt = jnp.minimum(pos0 + t * ts + rows + 1, w).astype(F32)
                o_ref[i * ts:(i + 1) * ts, :] = acc / cnt - cur


def _pool(z_r, row0, nbatch, seq, ts, bb, pos0, prefix=None, layer=0):
    nt = seq // ts
    use_prev = nt > 1
    tc = POOL_GW
    rb0 = row0 // (ts * bb)
    in_specs = [pl.BlockSpec((ts * bb, tc), lambda b, t, g: (rb0 + b * nt + t, g))]
    args = [z_r]
    if use_prev:
        per = ts // HALO
        in_specs.append(pl.BlockSpec(
            (HALO, tc), lambda b, t, g: (jnp.maximum((rb0 + b * nt + t) * per - 1, 0), g)))
        args.append(z_r)
    if prefix is not None:
        in_specs.append(pl.BlockSpec((None, bb, POOL_BUF, tc), lambda b, t, g: (layer, b, 0, g)))
        args.append(prefix)
    return pl.pallas_call(
        functools.partial(_pool_body, ts=ts, bb=bb, pos0=pos0, use_prev=use_prev, has_prefix=prefix is not None),
        out_shape=jax.ShapeDtypeStruct((nbatch * seq, D_MODEL), F32),
        grid=(nbatch // bb, nt, len(POOL_WINDOWS)),
        in_specs=in_specs,
        out_specs=pl.BlockSpec((ts * bb, tc), lambda b, t, g: (b * nt + t, g)),
        scratch_shapes=[pltpu.VMEM((HALO + ts, tc), F32)],
        compiler_params=_cparams(("parallel", "arbitrary", "arbitrary")),
        name="pool",
    )(*args)


def _mlstm_body(*refs, L, LP, HB, has_state, has_alias):
    refs = list(refs)
    (q_ref, k_ref, v_ref, zo_ref, zgm_ref, zic_ref, zfc_ref, zir_ref, zfr_ref, hn_ref) = refs[:10]
    refs = refs[10:]
    if has_state:
        c0_ref, n0_ref, m0_ref = refs[:3]
        refs = refs[3:]
    if has_alias:
        refs = refs[1:]
    h_ref, c_ref, n_ref, m_ref = refs
    hb = pl.program_id(1)
    c = pl.program_id(2)

    @pl.when(c == 0)
    def _():
        if has_state:
            c_ref[...] = c0_ref[...]
            n_ref[...] = n0_ref[...]
            m_ref[...] = m0_ref[...]
        else:
            c_ref[...] = jnp.zeros_like(c_ref)
            n_ref[...] = jnp.zeros_like(n_ref)
            m_ref[...] = jnp.zeros_like(m_ref)

    row_ids = lax.broadcasted_iota(jnp.int32, (L, LP), 0)
    lane_ids = lax.broadcasted_iota(jnp.int32, (L, LP), 1)
    tri = lane_ids <= row_ids
    tri_t = row_ids <= lane_ids
    head_lane = lax.broadcasted_iota(jnp.int32, (L, N_HEADS), 1)
    head_sub = lax.broadcasted_iota(jnp.int32, (N_HEADS, LP), 0)
    zi_c = zic_ref[...]
    zf_c = zfc_ref[...]
    zi_r = zir_ref[0]
    zf_r = zfr_ref[0]
    pad = LP - L

    def pad_rows(x):
        if pad == 0:
            return x
        return jnp.concatenate([x, jnp.zeros((pad, x.shape[1]), x.dtype)], axis=0)

    for j in range(HB):
        head = hb * HB + j
        logi_c = jnp.sum(jnp.where(head_lane == head, zi_c, 0.0), axis=1, keepdims=True)
        logf_c = _log_sigmoid(jnp.sum(jnp.where(head_lane == head, zf_c, 0.0), axis=1, keepdims=True))
        logi_r = jnp.sum(jnp.where(head_sub == head, zi_r, 0.0), axis=0, keepdims=True)
        logf_r = _log_sigmoid(jnp.sum(jnp.where(head_sub == head, zf_r, 0.0), axis=0, keepdims=True))

        b_c = jnp.sum(jnp.where(tri, logf_r, 0.0), axis=1, keepdims=True)
        b_r = jnp.sum(jnp.where(tri_t, logf_c, 0.0), axis=0, keepdims=True)
        b_last = jnp.sum(logf_r, axis=1, keepdims=True)
        a_c = logi_c - b_c
        a_r = logi_r - b_r
        m_prev = m_ref[0, j]
        big_m = jnp.maximum(m_prev, jnp.max(jnp.where(tri, a_r, -jnp.inf), axis=1, keepdims=True))
        m_last = jnp.maximum(m_prev, jnp.max(a_r, axis=1, keepdims=True))
        d = jnp.exp(jnp.where(tri, a_r - big_m, -jnp.inf))
        inter = jnp.exp(m_prev - big_m)

        q = q_ref[:, j * DK:(j + 1) * DK].astype(BF16)
        k = k_ref[:, j * DK:(j + 1) * DK]
        v = pad_rows(v_ref[:, j * DV:(j + 1) * DV]).astype(BF16)
        cmat = c_ref[0, j]
        n_row = n_ref[0, j]

        qk = lax.dot_general(q, pad_rows(k).astype(BF16), (((1,), (1,)), ((), ())), preferred_element_type=F32)
        s = qk * (d * (DK ** -0.5))
        num = inter * jnp.dot(q, cmat.astype(BF16), preferred_element_type=F32)
        num = num + jnp.dot(s.astype(BF16), v, preferred_element_type=F32)
        qn = jnp.sum(q.astype(F32) * n_row, axis=1, keepdims=True)
        den = inter * qn + jnp.sum(s, axis=1, keepdims=True)
        m_t = b_c + big_m
        hval = num * (1.0 / jnp.maximum(jnp.abs(den), jnp.exp(-m_t)))

        w_end = jnp.exp(a_c - m_last)
        decay = jnp.exp(m_prev - m_last)
        kw = k.astype(F32) * (w_end * (DK ** -0.5))
        kv = lax.dot_general(pad_rows(kw).astype(BF16), v, (((0,), (0,)), ((), ())), preferred_element_type=F32)
        c_ref[0, j] = decay * cmat + kv
        n_ref[0, j] = decay * n_row + jnp.sum(kw, axis=0, keepdims=True)
        m_ref[0, j] = b_last + m_last

        hnorm = hval * lax.rsqrt(jnp.mean(hval * hval, axis=-1, keepdims=True) + EPS)
        hnorm = hnorm * hn_ref[:, j * DV:(j + 1) * DV]
        gate = 1.0 / ((1.0 + jnp.exp(-zo_ref[:, j * DV:(j + 1) * DV])) * (1.0 + jnp.exp(-zgm_ref[:, j * DV:(j + 1) * DV])))
        h_ref[:, j * DV:(j + 1) * DV] = hnorm * gate


def _mlstm(qkv, z_o, z_r, zi_col, zf_col, zi_row, zf_row, head_norm, *, nbatch, seq, L, LP, HB, row0,
           state=None, c_stack=None):
    nc = seq // L
    nhb = N_HEADS // HB
    zrow0 = row0 // L
    in_specs = [
        pl.BlockSpec((L, HB * DK), lambda b, h, c: (b * nc + c, h)),
        pl.BlockSpec((L, HB * DK), lambda b, h, c: (b * nc + c, nhb + h)),
        pl.BlockSpec((L, HB * DV), lambda b, h, c: (b * nc + c, nhb + h)),
        pl.BlockSpec((L, HB * DV), lambda b, h, c: (zrow0 + b * nc + c, h)),
        pl.BlockSpec((L, HB * DV), lambda b, h, c: (zrow0 + b * nc + c, nhb + h)),
        pl.BlockSpec((L, N_HEADS), lambda b, h, c: (b * nc + c, 0)),
        pl.BlockSpec((L, N_HEADS), lambda b, h, c: (b * nc + c, 0)),
        pl.BlockSpec((1, N_HEADS, LP), lambda b, h, c: (b, 0, c)),
        pl.BlockSpec((1, N_HEADS, LP), lambda b, h, c: (b, 0, c)),
        pl.BlockSpec((1, HB * DV), lambda b, h, c: (0, h)),
    ]
    args = [qkv, qkv, qkv, z_o, z_r, zi_col, zf_col, zi_row, zf_row, head_norm]
    aliases = {}
    if state is not None:
        c0, n0, m0, layer = state
        in_specs += [
            pl.BlockSpec((None, 1, HB, DK, DV), lambda b, h, c: (layer, b, h, 0, 0)),
            pl.BlockSpec((None, 1, HB, 1, DK), lambda b, h, c: (layer, b, h, 0, 0)),
            pl.BlockSpec((None, 1, HB, 1, 1), lambda b, h, c: (layer, b, h, 0, 0)),
        ]
        args += [c0, n0, m0]
        c_shape = jax.ShapeDtypeStruct((DEPTH, nbatch, N_HEADS, DK, DV), F32)
        c_spec = pl.BlockSpec((None, 1, HB, DK, DV), lambda b, h, c: (layer, b, h, 0, 0))
        if c_stack is not None:
            in_specs.append(pl.BlockSpec(memory_space=pl.ANY))
            args.append(c_stack)
            aliases = {len(args) - 1: 1}
    else:
        c_shape = jax.ShapeDtypeStruct((nbatch, N_HEADS, DK, DV), F32)
        c_spec = pl.BlockSpec((1, HB, DK, DV), lambda b, h, c: (b, h, 0, 0))
    out_shape = (
        jax.ShapeDtypeStruct((nbatch * seq, N_HEADS * DV), F32),
        c_shape,
        jax.ShapeDtypeStruct((nbatch, N_HEADS, 1, DK), F32),
        jax.ShapeDtypeStruct((nbatch, N_HEADS, 1, 1), F32),
    )
    out_specs = (
        pl.BlockSpec((L, HB * DV), lambda b, h, c: (b * nc + c, h)),
        c_spec,
        pl.BlockSpec((1, HB, 1, DK), lambda b, h, c: (b, h, 0, 0)),
        pl.BlockSpec((1, HB, 1, 1), lambda b, h, c: (b, h, 0, 0)),
    )
    return pl.pallas_call(
        functools.partial(_mlstm_body, L=L, LP=LP, HB=HB, has_state=state is not None,
                          has_alias=c_stack is not None),
        out_shape=out_shape,
        grid=(nbatch, nhb, nc),
        in_specs=in_specs,
        out_specs=out_specs,
        input_output_aliases=aliases,
        compiler_params=_cparams(("parallel", "parallel", "arbitrary")),
        name="mlstm_state" if state is not None else "mlstm",
    )(*args)


def _time_on_lanes(z, nbatch, seq, lp, fill):
    zr = jnp.transpose(z.reshape(nbatch, seq, N_HEADS), (0, 2, 1))
    if seq < lp:
        zr = jnp.pad(zr, ((0, 0), (0, 0), (0, lp - seq)), constant_values=fill)
    return zr


@jax.jit
def _forward(x_prompt, x_sample, state_c, state_n, state_m, state_pool, norm_ff1, w_ff1_in, w_ff1_out,
             norm_mix, w_in, b_in, head_norm, w_pool, pool_scale, w_out, norm_ff2, w_ff2_in, w_ff2_out,
             norm_final):
    wu1 = w_ff1_in[:, :, D_FF:].astype(BF16)
    wu2 = w_ff2_in[:, :, D_FF:].astype(BF16)
    wo1 = _cast_pad_rows(w_ff1_out, D_FF_PAD, D_FF_PAD // 16)
    wo2 = _cast_pad_rows(w_ff2_out, D_FF_PAD, D_FF_PAD // 16)
    w_in_t = jnp.transpose(w_in, (0, 2, 1))
    b_all = b_in.reshape(DEPTH, 1, b_in.shape[1])
    b_qkv = b_all[:, :, COL_QKV:COL_QKV + N_QKV]
    b_o = b_all[:, :, COL_O:COL_O + D_MODEL]
    b_if = b_all[:, :, COL_IF:COL_IF + LANES]
    b_rest = b_all[:, :, COL_REST:]
    pool_scale3 = pool_scale.reshape(DEPTH, 1, D_MODEL)

    state_n5 = state_n.reshape(DEPTH, DEC_BATCH, N_HEADS, 1, DK)
    state_m5 = state_m.reshape(DEPTH, DEC_BATCH, N_HEADS, 1, 1)

    x = jnp.concatenate([x_prompt.reshape(T_PROMPT, D_MODEL), x_sample.reshape(T_SAMPLE, D_MODEL)], axis=0)
    cs_p, ns_p, ms_p, ps_p, ns_s, ms_s, ps_s = ([] for _ in range(7))
    c_stack = None
    for l in range(DEPTH):
        a = _ffn_in(_rmsnorm(x, norm_ff1[l], BF16), w_ff1_in, wu1, l)
        x = _ffn_out(a, wo1, l, x)

        hn = _rmsnorm(x, norm_mix[l], BF16)
        qkv = _in_proj(hn, w_in_t, b_qkv, l, COL_QKV, BF16, 512)
        z_o = _in_proj(hn, w_in_t, b_o, l, COL_O, F32, 512)
        gates = _in_proj(hn, w_in_t, b_if, l, COL_IF, F32, LANES)
        z_r = _in_proj(hn, w_in_t, b_rest, l, COL_REST, F32, 512)
        zi, zf = gates[:, :N_HEADS], gates[:, N_HEADS:2 * N_HEADS]
        hnorm_w = head_norm[l].reshape(1, N_HEADS * DV)

        hmg_p, c_p, n_p, m_p = _mlstm(
            qkv, z_o, z_r, zi[:T_PROMPT], zf[:T_PROMPT],
            _time_on_lanes(zi[:T_PROMPT], BATCH, SEQ, MLSTM_CHUNK, 0.0),
            _time_on_lanes(zf[:T_PROMPT], BATCH, SEQ, MLSTM_CHUNK, 0.0),
            hnorm_w, nbatch=BATCH, seq=SEQ, L=MLSTM_CHUNK, LP=MLSTM_CHUNK, HB=4, row0=0)
        hmg_s, c_stack, n_s, m_s = _mlstm(
            qkv[T_PROMPT:].astype(F32), z_o, z_r, zi[T_PROMPT:], zf[T_PROMPT:],
            _time_on_lanes(zi[T_PROMPT:], DEC_BATCH, DEC_SEQ, LANES, -jnp.inf),
            _time_on_lanes(zf[T_PROMPT:], DEC_BATCH, DEC_SEQ, LANES, jnp.inf),
            hnorm_w, nbatch=DEC_BATCH, seq=DEC_SEQ, L=DEC_SEQ, LP=LANES, HB=N_HEADS, row0=T_PROMPT,
            state=(state_c, state_n5, state_m5, l), c_stack=c_stack)

        pooled_p = _pool(z_r, 0, BATCH, SEQ, 256, 1, 0)
        pooled_s = _pool(z_r, T_PROMPT, DEC_BATCH, DEC_SEQ, DEC_SEQ, 16, PAST_LEN, prefix=state_pool, layer=l)

        merged = _pool_merge(pooled_p, pooled_s, w_pool, l, pool_scale3, z_r, hmg_p, hmg_s)
        x = _mm_resid(merged, w_out, l, x)

        a = _ffn_in(_rmsnorm(x, norm_ff2[l], BF16), w_ff2_in, wu2, l)
        x = _ffn_out(a, wo2, l, x)

        cs_p.append(c_p)
        ns_p.append(n_p.reshape(BATCH, N_HEADS, DK))
        ms_p.append(m_p.reshape(BATCH, N_HEADS))
        ps_p.append(jnp.stack([z_r[(b + 1) * SEQ - POOL_BUF:(b + 1) * SEQ, :D_MODEL] for b in range(BATCH)]))
        ns_s.append(n_s.reshape(DEC_BATCH, N_HEADS, DK))
        ms_s.append(m_s.reshape(DEC_BATCH, N_HEADS))
        zu_s = z_r[T_PROMPT:, :D_MODEL].reshape(DEC_BATCH, DEC_SEQ, D_MODEL)
        ps_s.append(jnp.concatenate([state_pool[l][:, DEC_SEQ:], zu_s], axis=1))

    y_p, y_s = _rmsnorm_split(x, norm_final)
    return (y_p.reshape(BATCH, SEQ, D_MODEL), y_s.reshape(DEC_BATCH, DEC_SEQ, D_MODEL),
            jnp.stack(cs_p), jnp.stack(ns_p), jnp.stack(ms_p), jnp.stack(ps_p),
            c_stack, jnp.stack(ns_s), jnp.stack(ms_s), jnp.stack(ps_s))


def kernel(x_prompt, x_sample, state_mlstm_C, state_mlstm_n, state_mlstm_m, state_pool, norm_ff1, w_ff1_in,
           w_ff1_out, norm_mix, w_in, b_in, head_norm, w_pool, pool_scale, w_out, norm_ff2, w_ff2_in, w_ff2_out,
           norm_final):
    return _forward(x_prompt, x_sample, state_mlstm_C, state_mlstm_n, state_mlstm_m, state_pool, norm_ff1,
                    w_ff1_in, w_ff1_out, norm_mix, w_in, b_in, head_norm, w_pool, pool_scale, w_out, norm_ff2,
                    w_ff2_in, w_ff2_out, norm_final)
```
